```python
import jax, jax.numpy as jnp
from jax import lax
import numpy as np

D_MODEL = 4096
BATCH = 2
SEQ = 8192
DEPTH = 2

CHUNK = 64
Q_BLOCK = 128

MIX_WIDTH = D_MODEL
HEAD_DIM = 128
SB_WIDTH = 3 * MIX_WIDTH // 8
CONV_WIDTH = MIX_WIDTH // 4
GLA_WIDTH = MIX_WIDTH - SB_WIDTH - CONV_WIDTH
SB_HEADS = SB_WIDTH // HEAD_DIM
SB_SCALE = HEAD_DIM ** -0.5
GLA_HEADS = 6
GLA_DV = GLA_WIDTH // GLA_HEADS
GLA_DK = GLA_DV // 2
GLA_KEY_WIDTH = GLA_HEADS * GLA_DK
GLA_RANK = 16
GLA_TAU = 16.0
GLA_SCALE = GLA_DK ** -0.5
CONV_K = 31
D_FF = ((8 * D_MODEL // 3 + 255) // 256) * 256
EPS = 1e-6

SPLIT_SIZES = (SB_WIDTH, SB_WIDTH, SB_WIDTH,
               CONV_WIDTH, CONV_WIDTH,
               GLA_KEY_WIDTH, GLA_KEY_WIDTH,
               GLA_WIDTH, GLA_WIDTH,
               GLA_RANK)
IN_WIDTH = sum(SPLIT_SIZES)

kernel_name = "hybrid_sb_conformer_gla_macaron"


def rmsnorm(x, g):
    xf = x.astype(jnp.float32)
    y = xf * lax.rsqrt(jnp.mean(xf * xf, axis=-1, keepdims=True) + EPS)
    return (y * g.astype(jnp.float32)).astype(x.dtype)


def layernorm(x, g, b):
    xf = x.astype(jnp.float32)
    mu = jnp.mean(xf, axis=-1, keepdims=True)
    var = jnp.mean(jnp.square(xf - mu), axis=-1, keepdims=True)
    y = (xf - mu) * lax.rsqrt(var + EPS)
    return (y * g.astype(jnp.float32) + b.astype(jnp.float32)).astype(x.dtype)


def swiglu_ffn(h, w_in, w_out):
    gate, up = jnp.split(h @ w_in, 2, axis=-1)
    return (jax.nn.silu(gate) * up) @ w_out


def stick_breaking_attention(q, k, v):
    B, H, S, Dh = q.shape
    nb = S // Q_BLOCK
    qb = q.reshape(B, H, nb, Q_BLOCK, Dh).transpose(2, 0, 1, 3, 4)
    key_pos = jnp.arange(S)

    def block(args):
        qi, bi = args
        z = jnp.einsum('bhqd,bhkd->bhqk', qi, k,
                       preferred_element_type=jnp.float32) * SB_SCALE
        q_pos = bi * Q_BLOCK + jnp.arange(Q_BLOCK)
        mask = key_pos[None, :] < q_pos[:, None]
        log_keep = jnp.where(mask, jax.nn.log_sigmoid(-z), 0.0)
        after = lax.cumsum(log_keep, axis=3, reverse=True) - log_keep
        a = jnp.where(mask, jnp.exp(jax.nn.log_sigmoid(z) + after), 0.0)
        return jnp.einsum('bhqk,bhkd->bhqd', a.astype(v.dtype), v)

    out = lax.map(block, (qb, jnp.arange(nb)))
    return out.transpose(1, 2, 0, 3, 4).reshape(B, H, S, Dh)


def gated_linear_attention(q, k, v, log_a):
    B, H, S, dk = q.shape
    dv = v.shape[-1]
    n = S // CHUNK

    def to_chunks(t):
        return t.astype(jnp.float32).reshape(B, H, n, CHUNK, t.shape[-1]).transpose(2, 0, 1, 3, 4)

    qc, kc, vc, gc = to_chunks(q * GLA_SCALE), to_chunks(k), to_chunks(v), to_chunks(log_a)
    causal = jnp.tril(jnp.ones((CHUNK, CHUNK), dtype=bool))

    def step(state, inp):
        qi, ki, vi, gi = inp
        b = jnp.cumsum(gi, axis=2)
        rel = b[:, :, :, None, :] - b[:, :, None, :, :]
        decay = jnp.exp(jnp.where(causal[None, None, :, :, None], rel, -jnp.inf))
        scores = jnp.einsum('bhtd,bhsd,bhtsd->bhts', qi, ki, decay)
        o = (jnp.einsum('bhts,bhsv->bhtv', scores, vi)
             + jnp.einsum('bhtd,bhdv->bhtv', qi * jnp.exp(b), state))
        b_last = b[:, :, -1:, :]
        state = (jnp.exp(b_last)[:, :, 0, :, None] * state
                 + jnp.einsum('bhsd,bhsv->bhdv', ki * jnp.exp(b_last - b), vi))
        return state, o

    state0 = jnp.zeros((B, H, dk, dv), jnp.float32)
    _, out = lax.scan(step, state0, (qc, kc, vc, gc))
    return out.transpose(1, 2, 0, 3, 4).reshape(B, H, S, dv)


def causal_depthwise_conv(u, w, b):
    C = u.shape[-1]
    y = lax.conv_general_dilated(u, w[:, None, :].astype(u.dtype), window_strides=(1,),
                                 padding=[(CONV_K - 1, 0)],
                                 dimension_numbers=('NWC', 'WIO', 'NWC'),
                                 feature_group_count=C)
    return y + b.astype(u.dtype)


def hybrid_mixer(h, w_in, sb_q_gain, sb_k_gain, conv_w, conv_b, conv_ln_g, conv_ln_b,
                 gla_gate_w, gla_gate_b, gla_out_gain, w_out):
    B, S, _ = h.shape
    offsets = np.cumsum(SPLIT_SIZES)[:-1].tolist()
    (sb_q, sb_k, sb_v, conv_val, conv_gate,
     gla_q, gla_k, gla_v, gla_r, gla_lr) = jnp.split(h @ w_in, offsets, axis=-1)

    def heads(t, nh):
        return t.reshape(B, S, nh, -1).transpose(0, 2, 1, 3)

    def merge(t):
        return t.transpose(0, 2, 1, 3).reshape(B, S, -1)

    qa = rmsnorm(heads(sb_q, SB_HEADS), sb_q_gain)
    ka = rmsnorm(heads(sb_k, SB_HEADS), sb_k_gain)
    ya = merge(stick_breaking_attention(qa, ka, heads(sb_v, SB_HEADS)))

    u = conv_val * jax.nn.sigmoid(conv_gate)
    u = causal_depthwise_conv(u, conv_w, conv_b)
    yb = jax.nn.silu(layernorm(u, conv_ln_g, conv_ln_b))

    gate_logit = (gla_lr @ gla_gate_w + gla_gate_b).astype(jnp.float32)
    log_a = jax.nn.log_sigmoid(gate_logit) / GLA_TAU
    o = gated_linear_attention(heads(gla_q, GLA_HEADS), heads(gla_k, GLA_HEADS),
                               heads(gla_v, GLA_HEADS), heads(log_a, GLA_HEADS))
    o = rmsnorm(o, gla_out_gain)
    yc = (merge(o) * jax.nn.silu(gla_r.astype(jnp.float32))).astype(h.dtype)

    y = jnp.concatenate([ya.astype(h.dtype), yb.astype(h.dtype), yc], axis=-1)
    return y @ w_out


def setup_inputs(seed: int = 0) -> dict:
    key = jax.random.key(seed)
    ks = jax.random.split(key, 24)
    f32 = jnp.float32

    def normal(k, shape, scale):
        return jax.random.normal(k, shape, f32) * scale

    def gain(k, shape):
        return 1.0 + 0.02 * jax.random.normal(k, shape, f32)

    return {
        "x": jax.random.normal(ks[0], (BATCH, SEQ, D_MODEL), f32),
        "ffn1_norm": gain(ks[1], (DEPTH, D_MODEL)),
        "ffn1_w_in": normal(ks[2], (DEPTH, D_MODEL, 2 * D_FF), D_MODEL ** -0.5),
        "ffn1_w_out": normal(ks[3], (DEPTH, D_FF, D_MODEL), D_FF ** -0.5),
        "mix_norm": gain(ks[4], (DEPTH, D_MODEL)),
        "mix_w_in": normal(ks[5], (DEPTH, D_MODEL, IN_WIDTH), D_MODEL ** -0.5),
        "sb_q_gain": gain(ks[6], (DEPTH, HEAD_DIM)),
        "sb_k_gain": gain(ks[7], (DEPTH, HEAD_DIM)),
        "conv_w": normal(ks[8], (DEPTH, CONV_K, CONV_WIDTH), CONV_K ** -0.5),
        "conv_b": normal(ks[9], (DEPTH, CONV_WIDTH), 0.02),
        "conv_ln_g": gain(ks[10], (DEPTH, CONV_WIDTH)),
        "conv_ln_b": normal(ks[11], (DEPTH, CONV_WIDTH), 0.02),
        "gla_gate_w": normal(ks[12], (DEPTH, GLA_RANK, GLA_KEY_WIDTH), GLA_RANK ** -0.5),
        "gla_gate_b": normal(ks[13], (DEPTH, GLA_KEY_WIDTH), 0.1),
        "gla_out_gain": gain(ks[14], (DEPTH, GLA_DV)),
        "mix_w_out": normal(ks[15], (DEPTH, MIX_WIDTH, D_MODEL), MIX_WIDTH ** -0.5),
        "ffn2_norm": gain(ks[16], (DEPTH, D_MODEL)),
        "ffn2_w_in": normal(ks[17], (DEPTH, D_MODEL, 2 * D_FF), D_MODEL ** -0.5),
        "ffn2_w_out": normal(ks[18], (DEPTH, D_FF, D_MODEL), D_FF ** -0.5),
        "final_norm": gain(ks[19], (DEPTH, D_MODEL)),
    }


def reference(x, ffn1_norm, ffn1_w_in, ffn1_w_out, mix_norm, mix_w_in, sb_q_gain, sb_k_gain,
              conv_w, conv_b, conv_ln_g, conv_ln_b, gla_gate_w, gla_gate_b, gla_out_gain,
              mix_w_out, ffn2_norm, ffn2_w_in, ffn2_w_out, final_norm):
    for l in range(DEPTH):
        x = x + 0.5 * swiglu_ffn(rmsnorm(x, ffn1_norm[l]), ffn1_w_in[l], ffn1_w_out[l])
        x = x + hybrid_mixer(rmsnorm(x, mix_norm[l]), mix_w_in[l], sb_q_gain[l], sb_k_gain[l],
                             conv_w[l], conv_b[l], conv_ln_g[l], conv_ln_b[l],
                             gla_gate_w[l], gla_gate_b[l], gla_out_gain[l], mix_w_out[l])
        x = x + 0.5 * swiglu_ffn(rmsnorm(x, ffn2_norm[l]), ffn2_w_in[l], ffn2_w_out[l])
        x = rmsnorm(x, final_norm[l])
    return x
```

```python
import functools

import jax
import jax.numpy as jnp
from jax import lax
from jax.experimental import pallas as pl
from jax.experimental.pallas import tpu as pltpu

F32 = jnp.float32
BF16 = jnp.bfloat16

HEAD_DIM = 128
GLA_DK = 128
GLA_DV = 256
GLA_RANK = 16
GLA_TAU = 16.0
GLA_CHUNK = 64
CONV_K = 31
EPS = 1e-6
SB_SCALE = HEAD_DIM ** -0.5
GLA_SCALE = GLA_DK ** -0.5

LANES = 128
SUBLANES = 8
VMEM_LIMIT_BYTES = 56 * 1024 * 1024

NEG_BIG = -1e30


def _params(*sem):
    return pltpu.CompilerParams(dimension_semantics=sem, vmem_limit_bytes=VMEM_LIMIT_BYTES)


def _tile(n, want):
    if n <= want:
        return n
    t = want
    while n % t:
        t -= SUBLANES
    return t


def _split_bf16(x, terms):
    out = []
    r = x
    for _ in range(terms):
        h = r.astype(BF16)
        out.append(h)
        r = r - h.astype(F32)
    return out


def _softplus(z):
    return jnp.maximum(z, 0.0) + jnp.log(1.0 + jnp.exp(-jnp.abs(z)))


def _sigmoid(z):
    return 1.0 / (1.0 + jnp.exp(-z))


def _rmsnorm_kernel(x_ref, g_ref, o_ref):
    x = x_ref[...]
    ms = jnp.mean(x * x, axis=-1, keepdims=True)
    o_ref[...] = (x * lax.rsqrt(ms + EPS) * g_ref[...]).astype(o_ref.dtype)


def _rmsnorm(x2, g, out_dtype):
    n, d = x2.shape
    tm = _tile(n, 256)
    return pl.pallas_call(
        _rmsnorm_kernel,
        grid=(n // tm,),
        in_specs=[pl.BlockSpec((tm, d), lambda i: (i, 0)),
                  pl.BlockSpec((1, d), lambda i: (0, 0))],
        out_specs=pl.BlockSpec((tm, d), lambda i: (i, 0)),
        out_shape=jax.ShapeDtypeStruct((n, d), out_dtype),
        compiler_params=_params("parallel"),
        name="rmsnorm",
    )(x2, g.reshape(1, d))


FFN_TF = 512


def _ffn_up_kernel(h_ref, w_ref, o_ref):
    r = jnp.dot(h_ref[...], w_ref[...], preferred_element_type=F32)
    tf = o_ref.shape[1]
    g = r[:, :tf]
    u = r[:, tf:]
    o_ref[...] = (g * _sigmoid(g) * u).astype(o_ref.dtype)


def _ffn_down_kernel(a_ref, w_ref, x_ref, o_ref):
    acc = jnp.dot(a_ref[...], w_ref[...], preferred_element_type=F32)
    o_ref[...] = x_ref[...] + 0.5 * acc


def _prep_ffn_weights(w_in, w_out):
    d, two_f = w_in.shape
    f = two_f // 2
    fp = -(-f // FFN_TF) * FFN_TF
    nj = fp // FFN_TF
    gate = jnp.pad(w_in[:, :f].astype(BF16), ((0, 0), (0, fp - f))).reshape(d, nj, 1, FFN_TF)
    up = jnp.pad(w_in[:, f:].astype(BF16), ((0, 0), (0, fp - f))).reshape(d, nj, 1, FFN_TF)
    w_gu = jnp.concatenate([gate, up], axis=2).reshape(d, 2 * fp)
    w_dn = jnp.pad(w_out.astype(BF16), ((0, fp - f), (0, 0)))
    return w_gu, w_dn


def _ffn(x2, h, w_gu, w_dn):
    n, d = x2.shape
    fp = w_dn.shape[0]
    tm = _tile(n, 1024)
    a = pl.pallas_call(
        _ffn_up_kernel,
        grid=(n // tm, fp // FFN_TF),
        in_specs=[pl.BlockSpec((tm, d), lambda i, j: (i, 0)),
                  pl.BlockSpec((d, 2 * FFN_TF), lambda i, j: (0, j))],
        out_specs=pl.BlockSpec((tm, FFN_TF), lambda i, j: (i, j)),
        out_shape=jax.ShapeDtypeStruct((n, fp), BF16),
        compiler_params=_params("parallel", "arbitrary"),
        name="ffn_up",
    )(h, w_gu)
    tm2 = _tile(n, 512)
    tn = _tile(d, 512)
    return pl.pallas_call(
        _ffn_down_kernel,
        grid=(n // tm2, d // tn),
        in_specs=[pl.BlockSpec((tm2, fp), lambda i, j: (i, 0)),
                  pl.BlockSpec((fp, tn), lambda i, j: (0, j)),
                  pl.BlockSpec((tm2, tn), lambda i, j: (i, j))],
        out_specs=pl.BlockSpec((tm2, tn), lambda i, j: (i, j)),
        out_shape=jax.ShapeDtypeStruct((n, d), F32),
        compiler_params=_params("parallel", "arbitrary"),
        name="ffn_down",
    )(a, w_dn, x2)


INPROJ_TN = 768


def _matmul_kernel(a_ref, w_ref, o_ref):
    o_ref[...] = jnp.dot(a_ref[...], w_ref[...], preferred_element_type=F32).astype(o_ref.dtype)


def _in_proj(h, w_in_p):
    n, d = h.shape
    pw = w_in_p.shape[1]
    tm = _tile(n, 1024)
    return pl.pallas_call(
        _matmul_kernel,
        grid=(n // tm, pw // INPROJ_TN),
        in_specs=[pl.BlockSpec((tm, d), lambda i, j: (i, 0)),
                  pl.BlockSpec((d, INPROJ_TN), lambda i, j: (0, j))],
        out_specs=pl.BlockSpec((tm, INPROJ_TN), lambda i, j: (i, j)),
        out_shape=jax.ShapeDtypeStruct((n, pw), F32),
        compiler_params=_params("parallel", "arbitrary"),
        name="mix_in_proj",
    )(h, w_in_p)


def _out_proj_kernel(ya_ref, yb_ref, yc_ref, wa_ref, wb_ref, wc_ref, x_ref, o_ref):
    acc = jnp.dot(ya_ref[...], wa_ref[...], preferred_element_type=F32)
    acc += jnp.dot(yb_ref[...], wb_ref[...], preferred_element_type=F32)
    acc += jnp.dot(yc_ref[...], wc_ref[...], preferred_element_type=F32)
    o_ref[...] = x_ref[...] + acc


def _out_proj(ya, yb, yc, wa, wb, wc, x2):
    n, d = x2.shape
    tm = _tile(n, 1024)
    tn = _tile(d, 512)
    ka, kb, kc = ya.shape[1], yb.shape[1], yc.shape[1]
    return pl.pallas_call(
        _out_proj_kernel,
        grid=(n // tm, d // tn),
        in_specs=[pl.BlockSpec((tm, ka), lambda i, j: (i, 0)),
                  pl.BlockSpec((tm, kb), lambda i, j: (i, 0)),
                  pl.BlockSpec((tm, kc), lambda i, j: (i, 0)),
                  pl.BlockSpec((ka, tn), lambda i, j: (0, j)),
                  pl.BlockSpec((kb, tn), lambda i, j: (0, j)),
                  pl.BlockSpec((kc, tn), lambda i, j: (0, j)),
                  pl.BlockSpec((tm, tn), lambda i, j: (i, j))],
        out_specs=pl.BlockSpec((tm, tn), lambda i, j: (i, j)),
        out_shape=jax.ShapeDtypeStruct((n, d), F32),
        compiler_params=_params("parallel", "arbitrary"),
        name="mix_out_proj",
    )(ya, yb, yc, wa, wb, wc, x2)


def _sb_prep_kernel(q_ref, k_ref, v_ref, qg_ref, kg_ref, qo_ref, ko_ref, vo_ref):
    def norm(x, g):
        ms = jnp.mean(x * x, axis=-1, keepdims=True)
        return x * lax.rsqrt(ms + EPS) * g

    qo_ref[...] = (norm(q_ref[...], qg_ref[...]) * SB_SCALE).astype(qo_ref.dtype)
    ko_ref[...] = norm(k_ref[...], kg_ref[...]).astype(ko_ref.dtype)
    vo_ref[...] = v_ref[...].astype(vo_ref.dtype)


def _sb_prep(p, heads, q_off, k_off, v_off, q_gain, k_gain):
    n = p.shape[0]
    tm = _tile(n, 1024)
    assert q_off % HEAD_DIM == 0 and k_off % HEAD_DIM == 0 and v_off % HEAD_DIM == 0
    blk = lambda off: pl.BlockSpec((tm, HEAD_DIM), lambda i, h: (i, off // HEAD_DIM + h))
    gain = pl.BlockSpec((1, HEAD_DIM), lambda i, h: (0, 0))
    out = pl.BlockSpec((tm, HEAD_DIM), lambda i, h: (i, h))
    shape = jax.ShapeDtypeStruct((n, heads * HEAD_DIM), BF16)
    return pl.pallas_call(
        _sb_prep_kernel,
        grid=(n // tm, heads),
        in_specs=[blk(q_off), blk(k_off), blk(v_off), gain, gain],
        out_specs=[out, out, out],
        out_shape=[shape, shape, shape],
        compiler_params=_params("parallel", "arbitrary"),
        name="sb_prep",
    )(p, p, p, q_gain.reshape(1, HEAD_DIM), k_gain.reshape(1, HEAD_DIM))


SB_T = 256


def _sb_attn_kernel(q_ref, k_ref, v_ref, o_ref):
    t = SB_T
    i = pl.program_id(2)
    q = q_ref[...]
    row = lax.broadcasted_iota(jnp.int32, (t, t), 0)
    col = lax.broadcasted_iota(jnp.int32, (t, t), 1)
    tri = jnp.where(row >= col, 1.0, 0.0).astype(BF16)

    def block(kb, carry, masked):
        acc, run = carry
        start = pl.multiple_of(kb * t, t)
        k = k_ref[pl.ds(start, t), :]
        v = v_ref[pl.ds(start, t), :]
        z = lax.dot_general(q, k, (((1,), (1,)), ((), ())), preferred_element_type=F32)
        sp = _softplus(z)
        if masked:
            keep = col < row
            sp = jnp.where(keep, sp, 0.0)
        hi, lo = _split_bf16(sp, 2)
        c = (jnp.dot(hi, tri, preferred_element_type=F32)
             + jnp.dot(lo, tri, preferred_element_type=F32))
        e = z - c - run
        if masked:
            e = jnp.where(keep, e, NEG_BIG)
        a = jnp.exp(e)
        acc = acc + jnp.dot(a.astype(BF16), v, preferred_element_type=F32)
        return acc, run + c[:, :1]

    carry = (jnp.zeros((t, HEAD_DIM), F32), jnp.zeros((t, 1), F32))
    carry = block(i, carry, True)
    acc, _ = lax.fori_loop(0, i, lambda n, cr: block(i - 1 - n, cr, False), carry)
    o_ref[...] = acc.astype(o_ref.dtype)


def _sb_attention(qn, kn, vn, batch, seq, heads):
    w = heads * HEAD_DIM
    q3, k3, v3 = (a.reshape(batch, seq, w) for a in (qn, kn, vn))
    assert seq % SB_T == 0
    out = pl.pallas_call(
        _sb_attn_kernel,
        grid=(batch, heads, seq // SB_T),
        in_specs=[pl.BlockSpec((None, SB_T, HEAD_DIM), lambda b, h, i: (b, i, h)),
                  pl.BlockSpec((None, seq, HEAD_DIM), lambda b, h, i: (b, 0, h)),
                  pl.BlockSpec((None, seq, HEAD_DIM), lambda b, h, i: (b, 0, h))],
        out_specs=pl.BlockSpec((None, SB_T, HEAD_DIM), lambda b, h, i: (b, i, h)),
        out_shape=jax.ShapeDtypeStruct((batch, seq, w), BF16),
        compiler_params=_params("parallel", "parallel", "arbitrary"),
        name="sb_attention",
    )(q3, k3, v3)
    return out.reshape(batch * seq, w)


CONV_TS = 256
CONV_HALO = 32


def _conv_kernel(val_ref, gate_ref, hval_ref, hgate_ref, w_ref, b_ref, lg_ref, lb_ref, o_ref, u_ref):
    ts = o_ref.shape[0]
    i = pl.program_id(1)
    halo = hval_ref[...] * _sigmoid(hgate_ref[...])
    u_ref[0:CONV_HALO, :] = jnp.where(i > 0, halo, 0.0)
    u_ref[CONV_HALO:, :] = val_ref[...] * _sigmoid(gate_ref[...])
    base = CONV_HALO - (CONV_K - 1)
    acc = jnp.zeros(o_ref.shape, F32) + b_ref[...]
    for k in range(CONV_K):
        acc = acc + w_ref[k:k + 1, :] * u_ref[base + k:base + k + ts, :]
    mu = jnp.mean(acc, axis=-1, keepdims=True)
    cen = acc - mu
    var = jnp.mean(cen * cen, axis=-1, keepdims=True)
    y = cen * lax.rsqrt(var + EPS) * lg_ref[...] + lb_ref[...]
    o_ref[...] = (y * _sigmoid(y)).astype(o_ref.dtype)


def _conv_module(p, batch, seq, val_off, gate_off, width, conv_w, conv_b, ln_g, ln_b):
    p3 = p.reshape(batch, seq, p.shape[1])
    ts = _tile(seq, CONV_TS)
    assert val_off % width == 0 and gate_off % width == 0 and ts % CONV_HALO == 0
    cv, cg = val_off // width, gate_off // width
    hpt = ts // CONV_HALO
    main = lambda c: pl.BlockSpec((None, ts, width), lambda b, i: (b, i, c))
    halo = lambda c: pl.BlockSpec((None, CONV_HALO, width),
                                  lambda b, i: (b, jnp.maximum(i * hpt - 1, 0), c))
    vec = lambda r: pl.BlockSpec((r, width), lambda b, i: (0, 0))
    out = pl.pallas_call(
        _conv_kernel,
        grid=(batch, seq // ts),
        in_specs=[main(cv), main(cg), halo(cv), halo(cg), vec(CONV_K), vec(1), vec(1), vec(1)],
        out_specs=pl.BlockSpec((None, ts, width), lambda b, i: (b, i, 0)),
        out_shape=jax.ShapeDtypeStruct((batch, seq, width), BF16),
        scratch_shapes=[pltpu.VMEM((CONV_HALO + ts, width), F32)],
        compiler_params=_params("parallel", "arbitrary"),
        name="conv_module",
    )(p3, p3, p3, p3, conv_w, conv_b.reshape(1, width), ln_g.reshape(1, width), ln_b.reshape(1, width))
    return out.reshape(batch * seq, width)


GLA_ROWS = 256


def _gla_kernel(q_ref, k_ref, v_ref, r_ref, lr_ref, gw_ref, gb_ref, og_ref, o_ref,
                state_ref, b_ref):
    c = GLA_CHUNK

    @pl.when(pl.program_id(2) == 0)
    def _():
        state_ref[...] = jnp.zeros_like(state_ref)

    lr_parts = _split_bf16(lr_ref[...], 2)
    gw_parts = _split_bf16(gw_ref[...], 2)
    logit = (jnp.dot(lr_parts[0], gw_parts[0], preferred_element_type=F32)
             + jnp.dot(lr_parts[1], gw_parts[0], preferred_element_type=F32)
             + jnp.dot(lr_parts[0], gw_parts[1], preferred_element_type=F32)) + gb_ref[...]
    log_a = -_softplus(-logit) * (1.0 / GLA_TAU)

    row = lax.broadcasted_iota(jnp.int32, (c, c), 0)
    col = lax.broadcasted_iota(jnp.int32, (c, c), 1)
    tril = jnp.where(row >= col, 1.0, 0.0).astype(BF16)
    ones = jnp.ones((GLA_DK, LANES), BF16)
    t_idx = lax.broadcasted_iota(jnp.int32, (c, GLA_DK), 0)
    lane = lax.broadcasted_iota(jnp.int32, (c, LANES), 1)

    for ci in range(q_ref.shape[0] // c):
        rows = pl.ds(ci * c, c)
        g = log_a[ci * c:(ci + 1) * c, :]
        b = sum(jnp.dot(tril, part, preferred_element_type=F32) for part in _split_bf16(g, 3))
        b_ref[...] = b
        q = q_ref[rows, :] * GLA_SCALE
        v = v_ref[rows, :].astype(BF16)

        def key_row(s, scores):
            ks = k_ref[pl.ds(ci * c + s, 1), :]
            bs = b_ref[pl.ds(s, 1), :]
            decay = jnp.exp(jnp.where(t_idx >= s, b - bs, NEG_BIG))
            prod = (q * ks * decay).astype(BF16)
            tot = jnp.dot(prod, ones, preferred_element_type=F32)
            return jnp.where(lane == s, tot, scores)

        scores = lax.fori_loop(0, c, key_row, jnp.zeros((c, LANES), F32))
        state = state_ref[...]
        o = jnp.dot(scores[:, :c].astype(BF16), v, preferred_element_type=F32)
        o = o + lax.dot_general((q * jnp.exp(b)).astype(BF16), state.astype(BF16),
                                (((1,), (1,)), ((), ())), preferred_element_type=F32)
        b_last = b[c - 1:c, :]
        k_dec = (k_ref[rows, :] * jnp.exp(b_last - b)).astype(BF16)
        state_ref[...] = jnp.exp(b_last) * state + lax.dot_general(
            v, k_dec, (((0,), (0,)), ((), ())), preferred_element_type=F32)

        ms = jnp.mean(o * o, axis=-1, keepdims=True)
        o = o * lax.rsqrt(ms + EPS) * og_ref[...]
        gate = r_ref[rows, :]
        o_ref[rows, :] = (o * gate * _sigmoid(gate)).astype(o_ref.dtype)


def _gla(p, batch, seq, heads, q_off, k_off, v_off, r_off, lr_off, gate_w_p, gate_b, out_gain):
    p3 = p.reshape(batch, seq, p.shape[1])
    rows = _tile(seq, GLA_ROWS)
    assert rows % GLA_CHUNK == 0
    assert q_off % GLA_DK == 0 and k_off % GLA_DK == 0 and lr_off % LANES == 0
    assert v_off % GLA_DV == 0 and r_off % GLA_DV == 0
    spec = lambda width, blk0: pl.BlockSpec((None, rows, width), lambda b, h, i: (b, i, blk0 + h))
    out = pl.pallas_call(
        _gla_kernel,
        grid=(batch, heads, seq // rows),
        in_specs=[spec(GLA_DK, q_off // GLA_DK), spec(GLA_DK, k_off // GLA_DK),
                  spec(GLA_DV, v_off // GLA_DV), spec(GLA_DV, r_off // GLA_DV),
                  pl.BlockSpec((None, rows, LANES), lambda b, h, i: (b, i, lr_off // LANES)),
                  pl.BlockSpec((LANES, GLA_DK), lambda b, h, i: (0, h)),
                  pl.BlockSpec((1, GLA_DK), lambda b, h, i: (0, h)),
                  pl.BlockSpec((1, GLA_DV), lambda b, h, i: (0, 0))],
        out_specs=pl.BlockSpec((None, rows, GLA_DV), lambda b, h, i: (b, i, h)),
        out_shape=jax.ShapeDtypeStruct((batch, seq, heads * GLA_DV), BF16),
        scratch_shapes=[pltpu.VMEM((GLA_DV, GLA_DK), F32), pltpu.VMEM((GLA_CHUNK, GLA_DK), F32)],
        compiler_params=_params("parallel", "parallel", "arbitrary"),
        name="gla",
    )(p3, p3, p3, p3, p3, gate_w_p, gate_b.reshape(1, -1), out_gain.reshape(1, GLA_DV))
    return out.reshape(batch * seq, heads * GLA_DV)


def _layout(d_model):
    sb_w = 3 * d_model // 8
    conv_w = d_model // 4
    gla_w = d_model - sb_w - conv_w
    gla_heads = gla_w // GLA_DV
    key_w = gla_heads * GLA_DK
    names = ("sb_q", "sb_k", "sb_v", "conv_val", "conv_gate", "gla_q", "gla_k", "gla_v", "gla_r", "gla_lr")
    sizes = (sb_w, sb_w, sb_w, conv_w, conv_w, key_w, key_w, gla_w, gla_w, GLA_RANK)
    src, o = {}, 0
    for nm, sz in zip(names, sizes):
        src[nm] = (o, sz)
        o += sz
    order = ("conv_val", "conv_gate", "gla_v", "gla_r", "sb_q", "sb_k", "sb_v", "gla_q", "gla_k", "gla_lr")
    dst, o = {}, 0
    for nm in order:
        dst[nm] = o
        o += src[nm][1]
    return dict(sb_w=sb_w, conv_w=conv_w, gla_w=gla_w, sb_heads=sb_w // HEAD_DIM,
                gla_heads=gla_heads, src=src, dst=dst, order=order, in_w=sum(sizes))


def _prep_in_proj_weight(w_in, lay):
    cols = [w_in[:, lay["src"][nm][0]:lay["src"][nm][0] + lay["src"][nm][1]] for nm in lay["order"]]
    w = jnp.concatenate(cols, axis=1).astype(BF16)
    pw = -(-(lay["dst"]["gla_lr"] + LANES) // INPROJ_TN) * INPROJ_TN
    return jnp.pad(w, ((0, 0), (0, pw - w.shape[1])))


def kernel(x, ffn1_norm, ffn1_w_in, ffn1_w_out, mix_norm, mix_w_in, sb_q_gain, sb_k_gain, conv_w, conv_b, conv_ln_g, conv_ln_b, gla_gate_w, gla_gate_b, gla_out_gain, mix_w_out, ffn2_norm, ffn2_w_in, ffn2_w_out, final_norm):
    batch, seq, d = x.shape
    depth = ffn1_norm.shape[0]
    lay = _layout(d)
    dst = lay["dst"]
    assert mix_w_in.shape[2] == lay["in_w"]

    x2 = x.reshape(batch * seq, d)
    for l in range(depth):
        w_gu, w_dn = _prep_ffn_weights(ffn1_w_in[l], ffn1_w_out[l])
        x2 = _ffn(x2, _rmsnorm(x2, ffn1_norm[l], BF16), w_gu, w_dn)

        p = _in_proj(_rmsnorm(x2, mix_norm[l], BF16), _prep_in_proj_weight(mix_w_in[l], lay))
        qn, kn, vn = _sb_prep(p, lay["sb_heads"], dst["sb_q"], dst["sb_k"], dst["sb_v"],
                              sb_q_gain[l], sb_k_gain[l])
        ya = _sb_attention(qn, kn, vn, batch, seq, lay["sb_heads"])
        yb = _conv_module(p, batch, seq, dst["conv_val"], dst["conv_gate"], lay["conv_w"],
                          conv_w[l], conv_b[l], conv_ln_g[l], conv_ln_b[l])
        gate_w_p = jnp.pad(gla_gate_w[l], ((0, LANES - GLA_RANK), (0, 0)))
        yc = _gla(p, batch, seq, lay["gla_heads"], dst["gla_q"], dst["gla_k"], dst["gla_v"],
                  dst["gla_r"], dst["gla_lr"], gate_w_p, gla_gate_b[l], gla_out_gain[l])
        w_o = mix_w_out[l].astype(BF16)
        sb_w, cw = lay["sb_w"], lay["conv_w"]
        x2 = _out_proj(ya, yb, yc, w_o[:sb_w], w_o[sb_w:sb_w + cw], w_o[sb_w + cw:], x2)

        w_gu, w_dn = _prep_ffn_weights(ffn2_w_in[l], ffn2_w_out[l])
        x2 = _ffn(x2, _rmsnorm(x2, ffn2_norm[l], BF16), w_gu, w_dn)
        x2 = _rmsnorm(x2, final_norm[l], F32)
    return x2.reshape(batch, seq, d)
```

```python
import functools

import jax
import jax.numpy as jnp
from jax import lax
from jax.experimental import pallas as pl
from jax.experimental.pallas import tpu as pltpu

F32 = jnp.float32
BF16 = jnp.bfloat16

HEAD_DIM = 128
GLA_DK = 128
GLA_DV = 256
GLA_RANK = 16
GLA_TAU = 16.0
GLA_CHUNK = 64
CONV_K = 31
EPS = 1e-6
SB_SCALE = HEAD_DIM ** -0.5
GLA_SCALE = GLA_DK ** -0.5

LANES = 128
SUBLANES = 8
VMEM_LIMIT_BYTES = 56 * 1024 * 1024

NEG_BIG = -1e30


def _params(*sem):
    return pltpu.CompilerParams(dimension_semantics=sem, vmem_limit_bytes=VMEM_LIMIT_BYTES)


def _tile(n, want):
    if n <= want:
        return n
    t = want
    while n % t:
        t -= SUBLANES
    return t


def _split_bf16(x, terms):
    out = []
    r = x
    for _ in range(terms):
        h = r.astype(BF16)
        out.append(h)
        r = r - h.astype(F32)
    return out


def _softplus(z):
    return jnp.maximum(z, 0.0) + jnp.log(1.0 + jnp.exp(-jnp.abs(z)))


def _sigmoid(z):
    return 1.0 / (1.0 + jnp.exp(-z))


def _rmsnorm_kernel(x_ref, g_ref, o_ref):
    x = x_ref[...]
    ms = jnp.mean(x * x, axis=-1, keepdims=True)
    o_ref[...] = (x * lax.rsqrt(ms + EPS) * g_ref[...]).astype(o_ref.dtype)


def _rmsnorm(x2, g, out_dtype):
    n, d = x2.shape
    tm = _tile(n, 256)
    return pl.pallas_call(
        _rmsnorm_kernel,
        grid=(n // tm,),
        in_specs=[pl.BlockSpec((tm, d), lambda i: (i, 0)),
                  pl.BlockSpec((1, d), lambda i: (0, 0))],
        out_specs=pl.BlockSpec((tm, d), lambda i: (i, 0)),
        out_shape=jax.ShapeDtypeStruct((n, d), out_dtype),
        compiler_params=_params("parallel"),
        name="rmsnorm",
    )(x2, g.reshape(1, d))


FFN_TF = 256


def _ffn_up_kernel(h_ref, wg_ref, wu_ref, o_ref):
    h = h_ref[...]
    g = jnp.dot(h, wg_ref[...], preferred_element_type=F32)
    u = jnp.dot(h, wu_ref[...], preferred_element_type=F32)
    o_ref[...] = (g * _sigmoid(g) * u).astype(o_ref.dtype)


def _ffn_down_kernel(a_ref, w_ref, x_ref, o_ref):
    acc = jnp.dot(a_ref[...], w_ref[...], preferred_element_type=F32)
    o_ref[...] = x_ref[...] + 0.5 * acc


def _ffn(x2, h, w_in, w_out):
    n, d = x2.shape
    f = w_out.shape[0]
    assert w_in.shape == (d, 2 * f) and f % FFN_TF == 0
    nj = f // FFN_TF
    tm = _tile(n, 2048)
    a = pl.pallas_call(
        _ffn_up_kernel,
        grid=(n // tm, nj),
        in_specs=[pl.BlockSpec((tm, d), lambda i, j: (i, 0)),
                  pl.BlockSpec((d, FFN_TF), lambda i, j: (0, j)),
                  pl.BlockSpec((d, FFN_TF), lambda i, j: (0, nj + j))],
        out_specs=pl.BlockSpec((tm, FFN_TF), lambda i, j: (i, j)),
        out_shape=jax.ShapeDtypeStruct((n, f), BF16),
        compiler_params=_params("parallel", "arbitrary"),
        name="ffn_up",
    )(h, w_in, w_in)
    fp, w_dn = f, w_out
    tm2 = _tile(n, 512)
    tn = _tile(d, 512)
    return pl.pallas_call(
        _ffn_down_kernel,
        grid=(n // tm2, d // tn),
        in_specs=[pl.BlockSpec((tm2, fp), lambda i, j: (i, 0)),
                  pl.BlockSpec((fp, tn), lambda i, j: (0, j)),
                  pl.BlockSpec((tm2, tn), lambda i, j: (i, j))],
        out_specs=pl.BlockSpec((tm2, tn), lambda i, j: (i, j)),
        out_shape=jax.ShapeDtypeStruct((n, d), F32),
        compiler_params=_params("parallel", "arbitrary"),
        name="ffn_down",
    )(a, w_dn, x2)


INPROJ_TN = 768


def _matmul_kernel(a_ref, w_ref, o_ref):
    o_ref[...] = jnp.dot(a_ref[...], w_ref[...], preferred_element_type=F32).astype(o_ref.dtype)


def _in_proj(h, w_in_p):
    n, d = h.shape
    pw = w_in_p.shape[1]
    tm = _tile(n, 1024)
    return pl.pallas_call(
        _matmul_kernel,
        grid=(n // tm, pw // INPROJ_TN),
        in_specs=[pl.BlockSpec((tm, d), lambda i, j: (i, 0)),
                  pl.BlockSpec((d, INPROJ_TN), lambda i, j: (0, j))],
        out_specs=pl.BlockSpec((tm, INPROJ_TN), lambda i, j: (i, j)),
        out_shape=jax.ShapeDtypeStruct((n, pw), F32),
        compiler_params=_params("parallel", "arbitrary"),
        name="mix_in_proj",
    )(h, w_in_p)


def _out_proj_kernel(ya_ref, yb_ref, yc_ref, w_ref, x_ref, o_ref, y_ref):
    ka, kb = ya_ref.shape[1], yb_ref.shape[1]

    @pl.when(pl.program_id(1) == 0)
    def _():
        y_ref[:, 0:ka] = ya_ref[...]
        y_ref[:, ka:ka + kb] = yb_ref[...]
        y_ref[:, ka + kb:] = yc_ref[...]

    o_ref[...] = x_ref[...] + jnp.dot(y_ref[...], w_ref[...], preferred_element_type=F32)


def _out_proj(ya, yb, yc, w, x2):
    n, d = x2.shape
    tm = _tile(n, 1024)
    tn = _tile(d, 512)
    ka, kb, kc = ya.shape[1], yb.shape[1], yc.shape[1]
    assert ka + kb + kc == w.shape[0]
    return pl.pallas_call(
        _out_proj_kernel,
        grid=(n // tm, d // tn),
        in_specs=[pl.BlockSpec((tm, ka), lambda i, j: (i, 0)),
                  pl.BlockSpec((tm, kb), lambda i, j: (i, 0)),
                  pl.BlockSpec((tm, kc), lambda i, j: (i, 0)),
                  pl.BlockSpec((w.shape[0], tn), lambda i, j: (0, j)),
                  pl.BlockSpec((tm, tn), lambda i, j: (i, j))],
        out_specs=pl.BlockSpec((tm, tn), lambda i, j: (i, j)),
        out_shape=jax.ShapeDtypeStruct((n, d), F32),
        scratch_shapes=[pltpu.VMEM((tm, w.shape[0]), BF16)],
        compiler_params=_params("parallel", "arbitrary"),
        name="mix_out_proj",
    )(ya, yb, yc, w, x2)


def _sb_prep_kernel(q_ref, k_ref, v_ref, qg_ref, kg_ref, qo_ref, ko_ref, vo_ref):
    def norm(x, g):
        ms = jnp.mean(x * x, axis=-1, keepdims=True)
        return x * lax.rsqrt(ms + EPS) * g

    qo_ref[...] = (norm(q_ref[...], qg_ref[...]) * SB_SCALE).astype(qo_ref.dtype)
    ko_ref[...] = norm(k_ref[...], kg_ref[...]).astype(ko_ref.dtype)
    vo_ref[...] = v_ref[...].astype(vo_ref.dtype)


def _sb_prep(p, heads, q_off, k_off, v_off, q_gain, k_gain):
    n = p.shape[0]
    tm = _tile(n, 1024)
    assert q_off % HEAD_DIM == 0 and k_off % HEAD_DIM == 0 and v_off % HEAD_DIM == 0
    blk = lambda off: pl.BlockSpec((tm, HEAD_DIM), lambda i, h: (i, off // HEAD_DIM + h))
    gain = pl.BlockSpec((1, HEAD_DIM), lambda i, h: (0, 0))
    out = pl.BlockSpec((tm, HEAD_DIM), lambda i, h: (i, h))
    shape = jax.ShapeDtypeStruct((n, heads * HEAD_DIM), BF16)
    return pl.pallas_call(
        _sb_prep_kernel,
        grid=(n // tm, heads),
        in_specs=[blk(q_off), blk(k_off), blk(v_off), gain, gain],
        out_specs=[out, out, out],
        out_shape=[shape, shape, shape],
        compiler_params=_params("parallel", "arbitrary"),
        name="sb_prep",
    )(p, p, p, q_gain.reshape(1, HEAD_DIM), k_gain.reshape(1, HEAD_DIM))


SB_T = 256
SB_DEAD = 105.0


def _sb_attn_kernel(q_ref, k_ref, v_ref, o_ref):
    t = SB_T
    i = pl.program_id(2)
    q = q_ref[...]
    row = lax.broadcasted_iota(jnp.int32, (t, t), 0)
    col = lax.broadcasted_iota(jnp.int32, (t, t), 1)
    tri = jnp.where(row >= col, 1.0, 0.0).astype(BF16)

    def block(kb, carry, masked):
        acc, run = carry
        start = pl.multiple_of(kb * t, t)
        k = k_ref[pl.ds(start, t), :]
        v = v_ref[pl.ds(start, t), :]
        z = lax.dot_general(q, k, (((1,), (1,)), ((), ())), preferred_element_type=F32)
        sp = _softplus(z)
        if masked:
            keep = col < row
            sp = jnp.where(keep, sp, 0.0)
        hi, lo = _split_bf16(sp, 2)
        c = (jnp.dot(hi, tri, preferred_element_type=F32)
             + jnp.dot(lo, tri, preferred_element_type=F32))
        e = z - c - run
        if masked:
            e = jnp.where(keep, e, NEG_BIG)
        a = jnp.exp(e)
        acc = acc + jnp.dot(a.astype(BF16), v, preferred_element_type=F32)
        return acc, run + c[:, :1]

    acc, run = block(i, (jnp.zeros((t, HEAD_DIM), F32), jnp.zeros((t, 1), F32)), True)

    def live(st):
        kb, _, _, run_min = st
        return jnp.logical_and(kb >= 0, run_min < SB_DEAD)

    def step(st):
        kb, acc, run, _ = st
        acc, run = block(kb, (acc, run), False)
        return kb - 1, acc, run, jnp.min(run)

    _, acc, _, _ = lax.while_loop(live, step, (i - 1, acc, run, jnp.min(run)))
    o_ref[...] = acc.astype(o_ref.dtype)


def _sb_attention(qn, kn, vn, batch, seq, heads):
    w = heads * HEAD_DIM
    q3, k3, v3 = (a.reshape(batch, seq, w) for a in (qn, kn, vn))
    assert seq % SB_T == 0
    out = pl.pallas_call(
        _sb_attn_kernel,
        grid=(batch, heads, seq // SB_T),
        in_specs=[pl.BlockSpec((None, SB_T, HEAD_DIM), lambda b, h, i: (b, i, h)),
                  pl.BlockSpec((None, seq, HEAD_DIM), lambda b, h, i: (b, 0, h)),
                  pl.BlockSpec((None, seq, HEAD_DIM), lambda b, h, i: (b, 0, h))],
        out_specs=pl.BlockSpec((None, SB_T, HEAD_DIM), lambda b, h, i: (b, i, h)),
        out_shape=jax.ShapeDtypeStruct((batch, seq, w), BF16),
        compiler_params=_params("parallel", "parallel", "arbitrary"),
        name="sb_attention",
    )(q3, k3, v3)
    return out.reshape(batch * seq, w)


CONV_TS = 256
CONV_HALO = 32


def _conv_kernel(val_ref, gate_ref, hval_ref, hgate_ref, w_ref, b_ref, lg_ref, lb_ref, o_ref, u_ref):
    ts = o_ref.shape[0]
    i = pl.program_id(1)
    halo = hval_ref[...] * _sigmoid(hgate_ref[...])
    u_ref[0:CONV_HALO, :] = jnp.where(i > 0, halo, 0.0)
    u_ref[CONV_HALO:, :] = val_ref[...] * _sigmoid(gate_ref[...])
    base = CONV_HALO - (CONV_K - 1)
    acc = jnp.zeros(o_ref.shape, F32) + b_ref[...]
    for k in range(CONV_K):
        acc = acc + w_ref[k:k + 1, :] * u_ref[base + k:base + k + ts, :]
    mu = jnp.mean(acc, axis=-1, keepdims=True)
    cen = acc - mu
    var = jnp.mean(cen * cen, axis=-1, keepdims=True)
    y = cen * lax.rsqrt(var + EPS) * lg_ref[...] + lb_ref[...]
    o_ref[...] = (y * _sigmoid(y)).astype(o_ref.dtype)


def _conv_module(p, batch, seq, val_off, gate_off, width, conv_w, conv_b, ln_g, ln_b):
    p3 = p.reshape(batch, seq, p.shape[1])
    ts = _tile(seq, CONV_TS)
    assert val_off % width == 0 and gate_off % width == 0 and ts % CONV_HALO == 0
    cv, cg = val_off // width, gate_off // width
    hpt = ts // CONV_HALO
    main = lambda c: pl.BlockSpec((None, ts, width), lambda b, i: (b, i, c))
    halo = lambda c: pl.BlockSpec((None, CONV_HALO, width),
                                  lambda b, i: (b, jnp.maximum(i * hpt - 1, 0), c))
    vec = lambda r: pl.BlockSpec((r, width), lambda b, i: (0, 0))
    out = pl.pallas_call(
        _conv_kernel,
        grid=(batch, seq // ts),
        in_specs=[main(cv), main(cg), halo(cv), halo(cg), vec(CONV_K), vec(1), vec(1), vec(1)],
        out_specs=pl.BlockSpec((None, ts, width), lambda b, i: (b, i, 0)),
        out_shape=jax.ShapeDtypeStruct((batch, seq, width), BF16),
        scratch_shapes=[pltpu.VMEM((CONV_HALO + ts, width), F32)],
        compiler_params=_params("parallel", "arbitrary"),
        name="conv_module",
    )(p3, p3, p3, p3, conv_w, conv_b.reshape(1, width), ln_g.reshape(1, width), ln_b.reshape(1, width))
    return out.reshape(batch * seq, width)


GLA_ROWS = 256
GLA_SUB = 16


def _gla_kernel(q_ref, k_ref, v_ref, r_ref, lr_ref, gw_ref, gb_ref, og_ref, o_ref, state_ref):
    c, sb = GLA_CHUNK, GLA_SUB
    rows = q_ref.shape[0]
    nb, spc = rows // sb, c // sb

    @pl.when(pl.program_id(2) == 0)
    def _():
        state_ref[...] = jnp.zeros_like(state_ref)

    lr_parts = _split_bf16(lr_ref[...], 2)
    gw_parts = _split_bf16(gw_ref[...], 2)
    logit = (jnp.dot(lr_parts[0], gw_parts[0], preferred_element_type=F32)
             + jnp.dot(lr_parts[1], gw_parts[0], preferred_element_type=F32)
             + jnp.dot(lr_parts[0], gw_parts[1], preferred_element_type=F32)) + gb_ref[...]
    g = -_softplus(-logit) * (1.0 / GLA_TAU)

    row = lax.broadcasted_iota(jnp.int32, (rows, rows), 0)
    col = lax.broadcasted_iota(jnp.int32, (rows, rows), 1)
    tril = jnp.where((row >= col) & (row // c == col // c), 1.0, 0.0).astype(BF16)
    b = sum(jnp.dot(tril, part, preferred_element_type=F32) for part in _split_bf16(g, 3))

    q = q_ref[...] * GLA_SCALE
    k = k_ref[...]
    b3, q3, k3 = (a.reshape(nb, sb, GLA_DK) for a in (b, q, k))
    r3 = (b3 - g.reshape(nb, sb, GLA_DK))[:, 0:1, :]
    q_sub = (q3 * jnp.exp(b3 - r3)).reshape(rows, GLA_DK).astype(BF16)

    ones = jnp.ones((GLA_DK, LANES), BF16)
    t_off = lax.broadcasted_iota(jnp.int32, (nb, sb, GLA_DK), 1)
    lane = lax.broadcasted_iota(jnp.int32, (rows, LANES), 1)
    ridx = lax.broadcasted_iota(jnp.int32, (rows, LANES), 0)
    sub_lane0 = (ridx % c) // sb * sb
    diag = jnp.zeros((rows, LANES), F32)
    for s in range(sb):
        decay = jnp.exp(jnp.where(t_off >= s, b3 - b3[:, s:s + 1, :], NEG_BIG))
        prod = (q3 * k3[:, s:s + 1, :] * decay).reshape(rows, GLA_DK).astype(BF16)
        tot = jnp.dot(prod, ones, preferred_element_type=F32)
        diag = jnp.where(lane == sub_lane0 + s, tot, diag)

    kidx = lax.broadcasted_iota(jnp.int32, (c, GLA_DK), 0)
    og = og_ref[...]
    for ci in range(rows // c):
        lo = ci * c
        bc, kc = b[lo:lo + c, :], k[lo:lo + c, :]
        vc = v_ref[lo:lo + c, :].astype(BF16)
        parts = [jnp.zeros((sb, c), F32)]
        for i in range(1, spc):
            r_i = r3[ci * spc + i]
            k_i = (kc * jnp.exp(jnp.where(kidx < i * sb, r_i - bc, NEG_BIG))).astype(BF16)
            parts.append(lax.dot_general(q_sub[lo + i * sb:lo + (i + 1) * sb, :], k_i,
                                         (((1,), (1,)), ((), ())), preferred_element_type=F32))
        scores = jnp.concatenate(parts, axis=0) + diag[lo:lo + c, :c]

        state = state_ref[...]
        o = jnp.dot(scores.astype(BF16), vc, preferred_element_type=F32)
        o = o + lax.dot_general((q[lo:lo + c, :] * jnp.exp(bc)).astype(BF16), state.astype(BF16),
                                (((1,), (1,)), ((), ())), preferred_element_type=F32)
        b_last = bc[c - 1:c, :]
        k_dec = (kc * jnp.exp(b_last - bc)).astype(BF16)
        state_ref[...] = jnp.exp(b_last) * state + lax.dot_general(
            vc, k_dec, (((0,), (0,)), ((), ())), preferred_element_type=F32)

        ms = jnp.mean(o * o, axis=-1, keepdims=True)
        o = o * lax.rsqrt(ms + EPS) * og
        gate = r_ref[lo:lo + c, :]
        o_ref[lo:lo + c, :] = (o * gate * _sigmoid(gate)).astype(o_ref.dtype)


def _gla(p, batch, seq, heads, q_off, k_off, v_off, r_off, lr_off, gate_w_p, gate_b, out_gain):
    p3 = p.reshape(batch, seq, p.shape[1])
    rows = _tile(seq, GLA_ROWS)
    assert rows % GLA_CHUNK == 0
    assert q_off % GLA_DK == 0 and k_off % GLA_DK == 0 and lr_off % LANES == 0
    assert v_off % GLA_DV == 0 and r_off % GLA_DV == 0
    spec = lambda width, blk0: pl.BlockSpec((None, rows, width), lambda b, h, i: (b, i, blk0 + h))
    out = pl.pallas_call(
        _gla_kernel,
        grid=(batch, heads, seq // rows),
        in_specs=[spec(GLA_DK, q_off // GLA_DK), spec(GLA_DK, k_off // GLA_DK),
                  spec(GLA_DV, v_off // GLA_DV), spec(GLA_DV, r_off // GLA_DV),
                  pl.BlockSpec((None, rows, LANES), lambda b, h, i: (b, i, lr_off // LANES)),
                  pl.BlockSpec((LANES, GLA_DK), lambda b, h, i: (0, h)),
                  pl.BlockSpec((1, GLA_DK), lambda b, h, i: (0, h)),
                  pl.BlockSpec((1, GLA_DV), lambda b, h, i: (0, 0))],
        out_specs=pl.BlockSpec((None, rows, GLA_DV), lambda b, h, i: (b, i, h)),
        out_shape=jax.ShapeDtypeStruct((batch, seq, heads * GLA_DV), BF16),
        scratch_shapes=[pltpu.VMEM((GLA_DV, GLA_DK), F32)],
        compiler_params=_params("parallel", "parallel", "arbitrary"),
        name="gla",
    )(p3, p3, p3, p3, p3, gate_w_p, gate_b.reshape(1, -1), out_gain.reshape(1, GLA_DV))
    return out.reshape(batch * seq, heads * GLA_DV)


def _layout(d_model):
    sb_w = 3 * d_model // 8
    conv_w = d_model // 4
    gla_w = d_model - sb_w - conv_w
    gla_heads = gla_w // GLA_DV
    key_w = gla_heads * GLA_DK
    names = ("sb_q", "sb_k", "sb_v", "conv_val", "conv_gate", "gla_q", "gla_k", "gla_v", "gla_r", "gla_lr")
    sizes = (sb_w, sb_w, sb_w, conv_w, conv_w, key_w, key_w, gla_w, gla_w, GLA_RANK)
    src, o = {}, 0
    for nm, sz in zip(names, sizes):
        src[nm] = (o, sz)
        o += sz
    order = ("conv_val", "conv_gate", "gla_v", "gla_r", "sb_q", "sb_k", "sb_v", "gla_q", "gla_k", "gla_lr")
    dst, o = {}, 0
    for nm in order:
        dst[nm] = o
        o += src[nm][1]
    return dict(sb_w=sb_w, conv_w=conv_w, gla_w=gla_w, sb_heads=sb_w // HEAD_DIM,
                gla_heads=gla_heads, src=src, dst=dst, order=order, in_w=sum(sizes))


def _prep_in_proj_weight(w_in, lay):
    cols = [w_in[:, lay["src"][nm][0]:lay["src"][nm][0] + lay["src"][nm][1]] for nm in lay["order"]]
    w = jnp.concatenate(cols, axis=1).astype(BF16)
    pw = -(-(lay["dst"]["gla_lr"] + LANES) // INPROJ_TN) * INPROJ_TN
    return jnp.pad(w, ((0, 0), (0, pw - w.shape[1])))


def kernel(x, ffn1_norm, ffn1_w_in, ffn1_w_out, mix_norm, mix_w_in, sb_q_gain, sb_k_gain, conv_w, conv_b, conv_ln_g, conv_ln_b, gla_gate_w, gla_gate_b, gla_out_gain, mix_w_out, ffn2_norm, ffn2_w_in, ffn2_w_out, final_norm):
    batch, seq, d = x.shape
    depth = ffn1_norm.shape[0]
    lay = _layout(d)
    dst = lay["dst"]
    assert mix_w_in.shape[2] == lay["in_w"]

    x2 = x.reshape(batch * seq, d)
    for l in range(depth):
        x2 = _ffn(x2, _rmsnorm(x2, ffn1_norm[l], BF16), ffn1_w_in[l].astype(BF16), ffn1_w_out[l].astype(BF16))

        p = _in_proj(_rmsnorm(x2, mix_norm[l], BF16), _prep_in_proj_weight(mix_w_in[l], lay))
        qn, kn, vn = _sb_prep(p, lay["sb_heads"], dst["sb_q"], dst["sb_k"], dst["sb_v"],
                              sb_q_gain[l], sb_k_gain[l])
        ya = _sb_attention(qn, kn, vn, batch, seq, lay["sb_heads"])
        yb = _conv_module(p, batch, seq, dst["conv_val"], dst["conv_gate"], lay["conv_w"],
                          conv_w[l], conv_b[l], conv_ln_g[l], conv_ln_b[l])
        gate_w_p = jnp.pad(gla_gate_w[l], ((0, LANES - GLA_RANK), (0, 0)))
        yc = _gla(p, batch, seq, lay["gla_heads"], dst["gla_q"], dst["gla_k"], dst["gla_v"],
                  dst["gla_r"], dst["gla_lr"], gate_w_p, gla_gate_b[l], gla_out_gain[l])
        x2 = _out_proj(ya, yb, yc, mix_w_out[l].astype(BF16), x2)

        x2 = _ffn(x2, _rmsnorm(x2, ffn2_norm[l], BF16), ffn2_w_in[l].astype(BF16), ffn2_w_out[l].astype(BF16))
        x2 = _rmsnorm(x2, final_norm[l], F32)
    return x2.reshape(batch, seq, d)
```

```python
import functools

import jax
import jax.numpy as jnp
from jax import lax
from jax.experimental import pallas as pl
from jax.experimental.pallas import tpu as pltpu

F32 = jnp.float32
BF16 = jnp.bfloat16

HEAD_DIM = 128
GLA_DK = 128
GLA_DV = 256
GLA_RANK = 16
GLA_TAU = 16.0
GLA_CHUNK = 64
CONV_K = 31
EPS = 1e-6
SB_SCALE = HEAD_DIM ** -0.5
GLA_SCALE = GLA_DK ** -0.5

LANES = 128
SUBLANES = 8
VMEM_LIMIT_BYTES = 56 * 1024 * 1024

NEG_BIG = -1e30


def _params(*sem):
    return pltpu.CompilerParams(dimension_semantics=sem, vmem_limit_bytes=VMEM_LIMIT_BYTES)


def _tile(n, want):
    if n <= want:
        return n
    t = want
    while n % t:
        t -= SUBLANES
    return t


def _split_bf16(x, terms):
    out = []
    r = x
    for _ in range(terms):
        h = r.astype(BF16)
        out.append(h)
        r = r - h.astype(F32)
    return out


def _softplus(z):
    return jnp.maximum(z, 0.0) + jnp.log(1.0 + jnp.exp(-jnp.abs(z)))


def _sigmoid(z):
    return 1.0 / (1.0 + jnp.exp(-z))


def _rmsnorm_kernel(x_ref, g_ref, o_ref):
    x = x_ref[...]
    ms = jnp.mean(x * x, axis=-1, keepdims=True)
    o_ref[...] = (x * lax.rsqrt(ms + EPS) * g_ref[...]).astype(o_ref.dtype)


def _rmsnorm(x2, g, out_dtype):
    n, d = x2.shape
    tm = _tile(n, 256)
    return pl.pallas_call(
        _rmsnorm_kernel,
        grid=(n // tm,),
        in_specs=[pl.BlockSpec((tm, d), lambda i: (i, 0)),
                  pl.BlockSpec((1, d), lambda i: (0, 0))],
        out_specs=pl.BlockSpec((tm, d), lambda i: (i, 0)),
        out_shape=jax.ShapeDtypeStruct((n, d), out_dtype),
        compiler_params=_params("parallel"),
        name="rmsnorm",
    )(x2, g.reshape(1, d))


FFN_TF = 256


def _ffn_up_kernel(h_ref, wg_ref, wu_ref, o_ref):
    h = h_ref[...]
    g = jnp.dot(h, wg_ref[...].astype(BF16), preferred_element_type=F32)
    u = jnp.dot(h, wu_ref[...].astype(BF16), preferred_element_type=F32)
    o_ref[...] = (g * _sigmoid(g) * u).astype(o_ref.dtype)


def _ffn_down_kernel(a_ref, w_ref, x_ref, o_ref):
    acc = jnp.dot(a_ref[...], w_ref[...], preferred_element_type=F32)
    o_ref[...] = x_ref[...] + 0.5 * acc


def _ffn(x2, h, w_in, w_out):
    n, d = x2.shape
    f = w_out.shape[0]
    assert w_in.shape == (d, 2 * f) and f % FFN_TF == 0
    nj = f // FFN_TF
    tm = _tile(n, 2048)
    a = pl.pallas_call(
        _ffn_up_kernel,
        grid=(n // tm, nj),
        in_specs=[pl.BlockSpec((tm, d), lambda i, j: (i, 0), pipeline_mode=pl.Buffered(1)),
                  pl.BlockSpec((d, FFN_TF), lambda i, j: (0, j)),
                  pl.BlockSpec((d, FFN_TF), lambda i, j: (0, nj + j))],
        out_specs=pl.BlockSpec((tm, FFN_TF), lambda i, j: (i, j)),
        out_shape=jax.ShapeDtypeStruct((n, f), BF16),
        compiler_params=_params("parallel", "arbitrary"),
        name="ffn_up",
    )(h, w_in, w_in)
    tm2 = _tile(n, 512)
    tn = _tile(d, 512)
    return pl.pallas_call(
        _ffn_down_kernel,
        grid=(n // tm2, d // tn),
        in_specs=[pl.BlockSpec((tm2, f), lambda i, j: (i, 0)),
                  pl.BlockSpec((f, tn), lambda i, j: (0, j)),
                  pl.BlockSpec((tm2, tn), lambda i, j: (i, j))],
        out_specs=pl.BlockSpec((tm2, tn), lambda i, j: (i, j)),
        out_shape=jax.ShapeDtypeStruct((n, d), F32),
        compiler_params=_params("parallel", "arbitrary"),
        name="ffn_down",
    )(a, w_out, x2)


INPROJ_TN = 256


def _in_proj_kernel(pt_ref, qt_ref, h_ref, w_ref, qg_ref, kg_ref, p_ref, qkv_ref, *,
                    sb_tiles, last_tile, last_valid):
    del pt_ref, qt_ref
    j = pl.program_id(1)
    acc = jnp.dot(h_ref[...], w_ref[...].astype(BF16), preferred_element_type=F32)

    def head_norm(gain):
        for c0 in range(0, INPROJ_TN, HEAD_DIM):
            x = acc[:, c0:c0 + HEAD_DIM]
            ms = jnp.mean(x * x, axis=-1, keepdims=True)
            qkv_ref[:, c0:c0 + HEAD_DIM] = (x * lax.rsqrt(ms + EPS) * gain).astype(qkv_ref.dtype)

    @pl.when(j < sb_tiles)
    def _():
        head_norm(qg_ref[...] * SB_SCALE)

    @pl.when(jnp.logical_and(j >= sb_tiles, j < 2 * sb_tiles))
    def _():
        head_norm(kg_ref[...])

    @pl.when(jnp.logical_and(j >= 2 * sb_tiles, j < 3 * sb_tiles))
    def _():
        qkv_ref[...] = acc.astype(qkv_ref.dtype)

    @pl.when(jnp.logical_and(j >= 3 * sb_tiles, j < last_tile))
    def _():
        p_ref[...] = acc

    @pl.when(j == last_tile)
    def _():
        lane = lax.broadcasted_iota(jnp.int32, acc.shape, 1)
        p_ref[...] = jnp.where(lane < last_valid, acc, 0.0)


def _in_proj(h, w_in, lay, q_gain, k_gain):
    n, d = h.shape
    src, dst = lay["src"], lay["dst"]
    tn = INPROJ_TN
    nt = -(-lay["in_w"] // tn)
    sb_tiles = lay["sb_w"] // tn
    assert all(src[nm][0] % tn == 0 for nm in src) and all(dst[nm] % tn == 0 for nm in dst)
    p_tile = [0] * nt
    for nm in dst:
        o, sz = src[nm]
        for t in range(-(-sz // tn)):
            p_tile[o // tn + t] = dst[nm] // tn + t
    q_tile = [min(t, 3 * sb_tiles - 1) for t in range(nt)]
    tm = _tile(n, 2048)
    kern = functools.partial(_in_proj_kernel, sb_tiles=sb_tiles, last_tile=nt - 1,
                             last_valid=lay["in_w"] - (nt - 1) * tn)
    grid_spec = pltpu.PrefetchScalarGridSpec(
        num_scalar_prefetch=2,
        grid=(n // tm, nt),
        in_specs=[pl.BlockSpec((tm, d), lambda i, j, pt, qt: (i, 0), pipeline_mode=pl.Buffered(1)),
                  pl.BlockSpec((d, tn), lambda i, j, pt, qt: (0, j)),
                  pl.BlockSpec((1, HEAD_DIM), lambda i, j, pt, qt: (0, 0)),
                  pl.BlockSpec((1, HEAD_DIM), lambda i, j, pt, qt: (0, 0))],
        out_specs=[pl.BlockSpec((tm, tn), lambda i, j, pt, qt: (i, pt[j])),
                   pl.BlockSpec((tm, tn), lambda i, j, pt, qt: (i, qt[j]))])
    return pl.pallas_call(
        kern,
        grid_spec=grid_spec,
        out_shape=[jax.ShapeDtypeStruct((n, lay["p_w"]), F32),
                   jax.ShapeDtypeStruct((n, 3 * lay["sb_w"]), BF16)],
        compiler_params=_params("parallel", "arbitrary"),
        name="mix_in_proj",
    )(jnp.asarray(p_tile, jnp.int32), jnp.asarray(q_tile, jnp.int32), h, w_in,
      q_gain.reshape(1, HEAD_DIM), k_gain.reshape(1, HEAD_DIM))


def _out_proj_kernel(ya_ref, yb_ref, yc_ref, w_ref, x_ref, o_ref, y_ref):
    ka, kb = ya_ref.shape[1], yb_ref.shape[1]

    @pl.when(pl.program_id(1) == 0)
    def _():
        y_ref[:, 0:ka] = ya_ref[...]
        y_ref[:, ka:ka + kb] = yb_ref[...]
        y_ref[:, ka + kb:] = yc_ref[...]

    o_ref[...] = x_ref[...] + jnp.dot(y_ref[...], w_ref[...], preferred_element_type=F32)


def _out_proj(ya, yb, yc, w, x2):
    n, d = x2.shape
    tm = _tile(n, 1024)
    tn = _tile(d, 512)
    ka, kb, kc = ya.shape[1], yb.shape[1], yc.shape[1]
    assert ka + kb + kc == w.shape[0]
    return pl.pallas_call(
        _out_proj_kernel,
        grid=(n // tm, d // tn),
        in_specs=[pl.BlockSpec((tm, ka), lambda i, j: (i, 0)),
                  pl.BlockSpec((tm, kb), lambda i, j: (i, 0)),
                  pl.BlockSpec((tm, kc), lambda i, j: (i, 0)),
                  pl.BlockSpec((w.shape[0], tn), lambda i, j: (0, j)),
                  pl.BlockSpec((tm, tn), lambda i, j: (i, j))],
        out_specs=pl.BlockSpec((tm, tn), lambda i, j: (i, j)),
        out_shape=jax.ShapeDtypeStruct((n, d), F32),
        scratch_shapes=[pltpu.VMEM((tm, w.shape[0]), BF16)],
        compiler_params=_params("parallel", "arbitrary"),
        name="mix_out_proj",
    )(ya, yb, yc, w, x2)


SB_T = 256
SB_DEAD = 105.0


def _sb_attn_kernel(q_ref, k_ref, v_ref, o_ref):
    t = SB_T
    i = pl.program_id(2)
    q = q_ref[...]
    row = lax.broadcasted_iota(jnp.int32, (t, t), 0)
    col = lax.broadcasted_iota(jnp.int32, (t, t), 1)
    tri = jnp.where(row >= col, 1.0, 0.0).astype(BF16)

    def block(kb, carry, limit):
        acc, run = carry
        start = pl.multiple_of(kb * t, t)
        k = k_ref[pl.ds(start, t), :]
        v = v_ref[pl.ds(start, t), :]
        z = lax.dot_general(q, k, (((1,), (1,)), ((), ())), preferred_element_type=F32)
        sp = _softplus(z)
        if limit is not None:
            keep = col < limit
            sp = jnp.where(keep, sp, 0.0)
        hi, lo = _split_bf16(sp, 2)
        c = (jnp.dot(hi, tri, preferred_element_type=F32)
             + jnp.dot(lo, tri, preferred_element_type=F32))
        e = z - c - run
        if limit is not None:
            e = jnp.where(keep, e, NEG_BIG)
        a = jnp.exp(e)
        acc = acc + jnp.dot(a.astype(BF16), v, preferred_element_type=F32)
        return acc, run + c[:, :1]

    carry = block(i, (jnp.zeros((t, HEAD_DIM), F32), jnp.zeros((t, 1), F32)), row)
    acc, run = block(jnp.maximum(i - 1, 0), carry, jnp.where(i >= 1, t, 0))

    def live(st):
        kb, _, _, run_min = st
        return jnp.logical_and(kb >= 0, run_min < SB_DEAD)

    def step(st):
        kb, acc, run, _ = st
        acc, run = block(kb, (acc, run), None)
        return kb - 1, acc, run, jnp.min(run)

    _, acc, _, _ = lax.while_loop(live, step, (i - 2, acc, run, jnp.min(run)))
    o_ref[...] = acc.astype(o_ref.dtype)


def _sb_attention(qkv, batch, seq, heads):
    w = heads * HEAD_DIM
    q3 = qkv.reshape(batch, seq, 3 * w)
    k3 = v3 = q3
    assert seq % SB_T == 0
    out = pl.pallas_call(
        _sb_attn_kernel,
        grid=(batch, heads, seq // SB_T),
        in_specs=[pl.BlockSpec((None, SB_T, HEAD_DIM), lambda b, h, i: (b, i, h)),
                  pl.BlockSpec((None, seq, HEAD_DIM), lambda b, h, i: (b, 0, heads + h)),
                  pl.BlockSpec((None, seq, HEAD_DIM), lambda b, h, i: (b, 0, 2 * heads + h))],
        out_specs=pl.BlockSpec((None, SB_T, HEAD_DIM), lambda b, h, i: (b, i, h)),
        out_shape=jax.ShapeDtypeStruct((batch, seq, w), BF16),
        compiler_params=_params("parallel", "parallel", "arbitrary"),
        name="sb_attention",
    )(q3, k3, v3)
    return out.reshape(batch * seq, w)


CONV_TS = 256
CONV_HALO = 32


def _conv_kernel(val_ref, gate_ref, hval_ref, hgate_ref, w_ref, b_ref, lg_ref, lb_ref, o_ref, u_ref):
    ts = o_ref.shape[0]
    i = pl.program_id(1)
    halo = hval_ref[...] * _sigmoid(hgate_ref[...])
    u_ref[0:CONV_HALO, :] = jnp.where(i > 0, halo, 0.0)
    u_ref[CONV_HALO:, :] = val_ref[...] * _sigmoid(gate_ref[...])
    base = CONV_HALO - (CONV_K - 1)
    acc = jnp.zeros(o_ref.shape, F32) + b_ref[...]
    for k in range(CONV_K):
        acc = acc + w_ref[k:k + 1, :] * u_ref[base + k:base + k + ts, :]
    mu = jnp.mean(acc, axis=-1, keepdims=True)
    cen = acc - mu
    var = jnp.mean(cen * cen, axis=-1, keepdims=True)
    y = cen * lax.rsqrt(var + EPS) * lg_ref[...] + lb_ref[...]
    o_ref[...] = (y * _sigmoid(y)).astype(o_ref.dtype)


def _conv_module(p, batch, seq, val_off, gate_off, width, conv_w, conv_b, ln_g, ln_b):
    p3 = p.reshape(batch, seq, p.shape[1])
    ts = _tile(seq, CONV_TS)
    assert val_off % width == 0 and gate_off % width == 0 and ts % CONV_HALO == 0
    cv, cg = val_off // width, gate_off // width
    hpt = ts // CONV_HALO
    main = lambda c: pl.BlockSpec((None, ts, width), lambda b, i: (b, i, c))
    halo = lambda c: pl.BlockSpec((None, CONV_HALO, width),
                                  lambda b, i: (b, jnp.maximum(i * hpt - 1, 0), c))
    vec = lambda r: pl.BlockSpec((r, width), lambda b, i: (0, 0))
    out = pl.pallas_call(
        _conv_kernel,
        grid=(batch, seq // ts),
        in_specs=[main(cv), main(cg), halo(cv), halo(cg), vec(CONV_K), vec(1), vec(1), vec(1)],
        out_specs=pl.BlockSpec((None, ts, width), lambda b, i: (b, i, 0)),
        out_shape=jax.ShapeDtypeStruct((batch, seq, width), BF16),
        scratch_shapes=[pltpu.VMEM((CONV_HALO + ts, width), F32)],
        compiler_params=_params("parallel", "arbitrary"),
        name="conv_module",
    )(p3, p3, p3, p3, conv_w, conv_b.reshape(1, width), ln_g.reshape(1, width), ln_b.reshape(1, width))
    return out.reshape(batch * seq, width)


GLA_ROWS = 256
GLA_SUB = 16


def _gla_kernel(q_ref, k_ref, v_ref, r_ref, lr_ref, gw_ref, gb_ref, og_ref, o_ref, state_ref):
    c, sb = GLA_CHUNK, GLA_SUB
    rows = q_ref.shape[0]
    nb, spc = rows // sb, c // sb

    @pl.when(pl.program_id(2) == 0)
    def _():
        state_ref[...] = jnp.zeros_like(state_ref)

    lr_parts = _split_bf16(lr_ref[...], 2)
    gw_parts = _split_bf16(gw_ref[...], 2)
    logit = (jnp.dot(lr_parts[0], gw_parts[0], preferred_element_type=F32)
             + jnp.dot(lr_parts[1], gw_parts[0], preferred_element_type=F32)
             + jnp.dot(lr_parts[0], gw_parts[1], preferred_element_type=F32)) + gb_ref[...]
    g = -_softplus(-logit) * (1.0 / GLA_TAU)

    row = lax.broadcasted_iota(jnp.int32, (rows, rows), 0)
    col = lax.broadcasted_iota(jnp.int32, (rows, rows), 1)
    tril = jnp.where((row >= col) & (row // c == col // c), 1.0, 0.0).astype(BF16)
    b = sum(jnp.dot(tril, part, preferred_element_type=F32) for part in _split_bf16(g, 3))

    q = q_ref[...] * GLA_SCALE
    k = k_ref[...]
    b3, q3, k3 = (a.reshape(nb, sb, GLA_DK) for a in (b, q, k))
    r3 = (b3 - g.reshape(nb, sb, GLA_DK))[:, 0:1, :]
    q_sub = (q3 * jnp.exp(b3 - r3)).reshape(rows, GLA_DK).astype(BF16)

    ones = jnp.ones((GLA_DK, LANES), BF16)
    t_off = lax.broadcasted_iota(jnp.int32, (nb, sb, GLA_DK), 1)
    lane = lax.broadcasted_iota(jnp.int32, (rows, LANES), 1)
    ridx = lax.broadcasted_iota(jnp.int32, (rows, LANES), 0)
    sub_lane0 = (ridx % c) // sb * sb
    diag = jnp.zeros((rows, LANES), F32)
    for s in range(sb):
        decay = jnp.exp(jnp.where(t_off >= s, b3 - b3[:, s:s + 1, :], NEG_BIG))
        prod = (q3 * k3[:, s:s + 1, :] * decay).reshape(rows, GLA_DK).astype(BF16)
        tot = jnp.dot(prod, ones, preferred_element_type=F32)
        diag = jnp.where(lane == sub_lane0 + s, tot, diag)

    kidx = lax.broadcasted_iota(jnp.int32, (c, GLA_DK), 0)
    og = og_ref[...]
    for ci in range(rows // c):
        lo = ci * c
        bc, kc = b[lo:lo + c, :], k[lo:lo + c, :]
        vc = v_ref[lo:lo + c, :].astype(BF16)
        parts = [jnp.zeros((sb, c), F32)]
        for i in range(1, spc):
            r_i = r3[ci * spc + i]
            k_i = (kc * jnp.exp(jnp.where(kidx < i * sb, r_i - bc, NEG_BIG))).astype(BF16)
            parts.append(lax.dot_general(q_sub[lo + i * sb:lo + (i + 1) * sb, :], k_i,
                                         (((1,), (1,)), ((), ())), preferred_element_type=F32))
        scores = jnp.concatenate(parts, axis=0) + diag[lo:lo + c, :c]

        state = state_ref[...]
        o = jnp.dot(scores.astype(BF16), vc, preferred_element_type=F32)
        o = o + lax.dot_general((q[lo:lo + c, :] * jnp.exp(bc)).astype(BF16), state.astype(BF16),
                                (((1,), (1,)), ((), ())), preferred_element_type=F32)
        b_last = bc[c - 1:c, :]
        k_dec = (kc * jnp.exp(b_last - bc)).astype(BF16)
        state_ref[...] = jnp.exp(b_last) * state + lax.dot_general(
            vc, k_dec, (((0,), (0,)), ((), ())), preferred_element_type=F32)

        ms = jnp.mean(o * o, axis=-1, keepdims=True)
        o = o * lax.rsqrt(ms + EPS) * og
        gate = r_ref[lo:lo + c, :]
        o_ref[lo:lo + c, :] = (o * gate * _sigmoid(gate)).astype(o_ref.dtype)


def _gla(p, batch, seq, heads, q_off, k_off, v_off, r_off, lr_off, gate_w_p, gate_b, out_gain):
    p3 = p.reshape(batch, seq, p.shape[1])
    rows = _tile(seq, GLA_ROWS)
    assert rows % GLA_CHUNK == 0
    assert q_off % GLA_DK == 0 and k_off % GLA_DK == 0 and lr_off % LANES == 0
    assert v_off % GLA_DV == 0 and r_off % GLA_DV == 0
    spec = lambda width, blk0: pl.BlockSpec((None, rows, width), lambda b, h, i: (b, i, blk0 + h))
    out = pl.pallas_call(
        _gla_kernel,
        grid=(batch, heads, seq // rows),
        in_specs=[spec(GLA_DK, q_off // GLA_DK), spec(GLA_DK, k_off // GLA_DK),
                  spec(GLA_DV, v_off // GLA_DV), spec(GLA_DV, r_off // GLA_DV),
                  pl.BlockSpec((None, rows, LANES), lambda b, h, i: (b, i, lr_off // LANES)),
                  pl.BlockSpec((LANES, GLA_DK), lambda b, h, i: (0, h)),
                  pl.BlockSpec((1, GLA_DK), lambda b, h, i: (0, h)),
                  pl.BlockSpec((1, GLA_DV), lambda b, h, i: (0, 0))],
        out_specs=pl.BlockSpec((None, rows, GLA_DV), lambda b, h, i: (b, i, h)),
        out_shape=jax.ShapeDtypeStruct((batch, seq, heads * GLA_DV), BF16),
        scratch_shapes=[pltpu.VMEM((GLA_DV, GLA_DK), F32)],
        compiler_params=_params("parallel", "parallel", "arbitrary"),
        name="gla",
    )(p3, p3, p3, p3, p3, gate_w_p, gate_b.reshape(1, -1), out_gain.reshape(1, GLA_DV))
    return out.reshape(batch * seq, heads * GLA_DV)


def _layout(d_model):
    sb_w = 3 * d_model // 8
    conv_w = d_model // 4
    gla_w = d_model - sb_w - conv_w
    gla_heads = gla_w // GLA_DV
    key_w = gla_heads * GLA_DK
    names = ("sb_q", "sb_k", "sb_v", "conv_val", "conv_gate", "gla_q", "gla_k", "gla_v", "gla_r", "gla_lr")
    sizes = (sb_w, sb_w, sb_w, conv_w, conv_w, key_w, key_w, gla_w, gla_w, GLA_RANK)
    src, o = {}, 0
    for nm, sz in zip(names, sizes):
        src[nm] = (o, sz)
        o += sz
    order = ("conv_val", "conv_gate", "gla_v", "gla_r", "gla_q", "gla_k", "gla_lr")
    dst, o = {}, 0
    for nm in order:
        dst[nm] = o
        o += -(-src[nm][1] // INPROJ_TN) * INPROJ_TN
    return dict(sb_w=sb_w, conv_w=conv_w, gla_w=gla_w, sb_heads=sb_w // HEAD_DIM,
                gla_heads=gla_heads, src=src, dst=dst, p_w=o, in_w=sum(sizes))


def kernel(x, ffn1_norm, ffn1_w_in, ffn1_w_out, mix_norm, mix_w_in, sb_q_gain, sb_k_gain, conv_w, conv_b, conv_ln_g, conv_ln_b, gla_gate_w, gla_gate_b, gla_out_gain, mix_w_out, ffn2_norm, ffn2_w_in, ffn2_w_out, final_norm):
    batch, seq, d = x.shape
    depth = ffn1_norm.shape[0]
    lay = _layout(d)
    dst = lay["dst"]
    assert mix_w_in.shape[2] == lay["in_w"]

    x2 = x.reshape(batch * seq, d)
    for l in range(depth):
        x2 = _ffn(x2, _rmsnorm(x2, ffn1_norm[l], BF16), ffn1_w_in[l], ffn1_w_out[l].astype(BF16))

        p, qkv = _in_proj(_rmsnorm(x2, mix_norm[l], BF16), mix_w_in[l], lay, sb_q_gain[l], sb_k_gain[l])
        ya = _sb_attention(qkv, batch, seq, lay["sb_heads"])
        yb = _conv_module(p, batch, seq, dst["conv_val"], dst["conv_gate"], lay["conv_w"],
                          conv_w[l], conv_b[l], conv_ln_g[l], conv_ln_b[l])
        gate_w_p = jnp.pad(gla_gate_w[l], ((0, LANES - GLA_RANK), (0, 0)))
        yc = _gla(p, batch, seq, lay["gla_heads"], dst["gla_q"], dst["gla_k"], dst["gla_v"],
                  dst["gla_r"], dst["gla_lr"], gate_w_p, gla_gate_b[l], gla_out_gain[l])
        x2 = _out_proj(ya, yb, yc, mix_w_out[l].astype(BF16), x2)

        x2 = _ffn(x2, _rmsnorm(x2, ffn2_norm[l], BF16), ffn2_w_in[l], ffn2_w_out[l].astype(BF16))
        x2 = _rmsnorm(x2, final_norm[l], F32)
    return x2.reshape(batch, seq, d)
```

```python
import functools

import jax
import jax.numpy as jnp
from jax import lax
from jax.experimental import pallas as pl
from jax.experimental.pallas import tpu as pltpu

F32 = jnp.float32
BF16 = jnp.bfloat16

HEAD_DIM = 128
GLA_DK = 128
GLA_DV = 256
GLA_RANK = 16
GLA_TAU = 16.0
GLA_CHUNK = 64
CONV_K = 31
EPS = 1e-6
SB_SCALE = HEAD_DIM ** -0.5
GLA_SCALE = GLA_DK ** -0.5

LANES = 128
SUBLANES = 8
VMEM_LIMIT_BYTES = 56 * 1024 * 1024

NEG_BIG = -1e30


def _params(*sem):
    return pltpu.CompilerParams(dimension_semantics=sem, vmem_limit_bytes=VMEM_LIMIT_BYTES)


def _tile(n, want):
    if n <= want:
        return n
    t = want
    while n % t:
        t -= SUBLANES
    return t


def _split_bf16(x, terms):
    out = []
    r = x
    for _ in range(terms):
        h = r.astype(BF16)
        out.append(h)
        r = r - h.astype(F32)
    return out


def _softplus(z):
    return jnp.maximum(z, 0.0) + jnp.log(1.0 + jnp.exp(-jnp.abs(z)))


def _sigmoid(z):
    return 1.0 / (1.0 + jnp.exp(-z))


def _row_scale(ss_ref, d):
    return lax.rsqrt(ss_ref[:, 0:1] * (1.0 / d) + EPS)


def _store_residual(y, g_ref, o_ref, xg_ref, ss_ref):
    o_ref[...] = y
    if xg_ref is None:
        return
    xg_ref[...] = (y * g_ref[...]).astype(xg_ref.dtype)
    part = jnp.broadcast_to(jnp.sum(y * y, axis=-1, keepdims=True), ss_ref.shape)
    j = pl.program_id(1)

    @pl.when(j == 0)
    def _():
        ss_ref[...] = part

    @pl.when(j > 0)
    def _():
        ss_ref[...] += part


def _residual_out(n, d, tm, tn, with_norm):
    specs = [pl.BlockSpec((tm, tn), lambda i, j: (i, j))]
    shapes = [jax.ShapeDtypeStruct((n, d), F32)]
    if with_norm:
        specs += [pl.BlockSpec((tm, tn), lambda i, j: (i, j)), pl.BlockSpec((tm, LANES), lambda i, j: (i, 0))]
        shapes += [jax.ShapeDtypeStruct((n, d), BF16), jax.ShapeDtypeStruct((n, LANES), F32)]
    return specs, shapes


def _norm_kernel(*refs, d, with_out, with_pair):
    refs = list(refs)
    x_ref = refs.pop(0)
    x = x_ref[...]
    if with_out:
        g_ref = refs.pop(0)
    if with_pair:
        gn_ref = refs.pop(0)
    if with_out:
        o_ref = refs.pop(0)
        ms = jnp.mean(x * x, axis=-1, keepdims=True)
        x = x * lax.rsqrt(ms + EPS) * g_ref[...]
        o_ref[...] = x
    if with_pair:
        xg_ref, ss_ref = refs
        xg_ref[...] = (x * gn_ref[...]).astype(xg_ref.dtype)
        ss_ref[...] = jnp.broadcast_to(jnp.sum(x * x, axis=-1, keepdims=True), ss_ref.shape)


def _norm(x2, g_out, g_pair):
    n, d = x2.shape
    tm = _tile(n, 256)
    row = pl.BlockSpec((tm, d), lambda i: (i, 0))
    vec = pl.BlockSpec((1, d), lambda i: (0, 0))
    ins, in_specs, out_specs, out_shape = [x2], [row], [], []
    if g_out is not None:
        ins.append(g_out.reshape(1, d)); in_specs.append(vec)
        out_specs.append(row); out_shape.append(jax.ShapeDtypeStruct((n, d), F32))
    if g_pair is not None:
        ins.append(g_pair.reshape(1, d)); in_specs.append(vec)
        out_specs += [row, pl.BlockSpec((tm, LANES), lambda i: (i, 0))]
        out_shape += [jax.ShapeDtypeStruct((n, d), BF16), jax.ShapeDtypeStruct((n, LANES), F32)]
    return pl.pallas_call(
        functools.partial(_norm_kernel, d=d, with_out=g_out is not None, with_pair=g_pair is not None),
        grid=(n // tm,),
        in_specs=in_specs, out_specs=out_specs, out_shape=out_shape,
        compiler_params=_params("parallel"),
        name="rmsnorm",
    )(*ins)


FFN_TF = 256


def _ffn_up_kernel(xg_ref, ss_ref, wg_ref, wu_ref, o_ref):
    xg = xg_ref[...]
    s = _row_scale(ss_ref, xg.shape[1])
    g = s * jnp.dot(xg, wg_ref[...].astype(BF16), preferred_element_type=F32)
    u = s * jnp.dot(xg, wu_ref[...].astype(BF16), preferred_element_type=F32)
    o_ref[...] = (g * _sigmoid(g) * u).astype(o_ref.dtype)


def _ffn_down_kernel(a_ref, w_ref, x_ref, *rest):
    g_ref, o_ref, xg_ref, ss_ref = rest if len(rest) == 4 else (None, rest[0], None, None)
    acc = jnp.dot(a_ref[...], w_ref[...], preferred_element_type=F32)
    _store_residual(x_ref[...] + 0.5 * acc, g_ref, o_ref, xg_ref, ss_ref)


def _ffn(x2, xg, ss, w_in, w_out, layer, g_next):
    n, d = x2.shape
    f = w_out.shape[1]
    assert w_in.shape[1:] == (d, 2 * f) and f % FFN_TF == 0
    nj = f // FFN_TF
    tm = _tile(n, 2048)
    a = pl.pallas_call(
        _ffn_up_kernel,
        grid=(n // tm, nj),
        in_specs=[pl.BlockSpec((tm, d), lambda i, j: (i, 0), pipeline_mode=pl.Buffered(1)),
                  pl.BlockSpec((tm, LANES), lambda i, j: (i, 0)),
                  pl.BlockSpec((None, d, FFN_TF), lambda i, j: (layer, 0, j)),
                  pl.BlockSpec((None, d, FFN_TF), lambda i, j: (layer, 0, nj + j))],
        out_specs=pl.BlockSpec((tm, FFN_TF), lambda i, j: (i, j)),
        out_shape=jax.ShapeDtypeStruct((n, f), BF16),
        compiler_params=_params("parallel", "arbitrary"),
        name="ffn_up",
    )(xg, ss, w_in, w_in)
    tm2 = _tile(n, 512)
    tn = _tile(d, 512)
    with_norm = g_next is not None
    out_specs, out_shape = _residual_out(n, d, tm2, tn, with_norm)
    ins = [a, w_out, x2] + ([g_next.reshape(1, d)] if with_norm else [])
    in_specs = [pl.BlockSpec((tm2, f), lambda i, j: (i, 0)),
                pl.BlockSpec((None, f, tn), lambda i, j: (layer, 0, j)),
                pl.BlockSpec((tm2, tn), lambda i, j: (i, j))]
    if with_norm:
        in_specs.append(pl.BlockSpec((1, tn), lambda i, j: (0, j)))
    return pl.pallas_call(
        _ffn_down_kernel,
        grid=(n // tm2, d // tn),
        in_specs=in_specs, out_specs=out_specs, out_shape=out_shape,
        compiler_params=_params("parallel", "arbitrary"),
        name="ffn_down",
    )(*ins)


INPROJ_TN = 256


def _in_proj_kernel(pt_ref, qt_ref, xg_ref, ss_ref, w_ref, qg_ref, kg_ref, p_ref, qkv_ref, *,
                    sb_tiles, last_tile, last_valid):
    del pt_ref, qt_ref
    j = pl.program_id(1)
    xg = xg_ref[...]
    acc = _row_scale(ss_ref, xg.shape[1]) * jnp.dot(xg, w_ref[...].astype(BF16),
                                                    preferred_element_type=F32)

    def head_norm(gain):
        for c0 in range(0, INPROJ_TN, HEAD_DIM):
            x = acc[:, c0:c0 + HEAD_DIM]
            ms = jnp.mean(x * x, axis=-1, keepdims=True)
            qkv_ref[:, c0:c0 + HEAD_DIM] = (x * lax.rsqrt(ms + EPS) * gain).astype(qkv_ref.dtype)

    @pl.when(j < sb_tiles)
    def _():
        head_norm(qg_ref[...] * SB_SCALE)

    @pl.when(jnp.logical_and(j >= sb_tiles, j < 2 * sb_tiles))
    def _():
        head_norm(kg_ref[...])

    @pl.when(jnp.logical_and(j >= 2 * sb_tiles, j < 3 * sb_tiles))
    def _():
        qkv_ref[...] = acc.astype(qkv_ref.dtype)

    @pl.when(jnp.logical_and(j >= 3 * sb_tiles, j < last_tile))
    def _():
        p_ref[...] = acc

    @pl.when(j == last_tile)
    def _():
        lane = lax.broadcasted_iota(jnp.int32, acc.shape, 1)
        p_ref[...] = jnp.where(lane < last_valid, acc, 0.0)


def _in_proj(xg, ss, w_in, layer, lay, q_gain, k_gain):
    n, d = xg.shape
    src, dst = lay["src"], lay["dst"]
    tn = INPROJ_TN
    nt = -(-lay["in_w"] // tn)
    sb_tiles = lay["sb_w"] // tn
    assert all(src[nm][0] % tn == 0 for nm in src) and all(dst[nm] % tn == 0 for nm in dst)
    p_tile = [0] * nt
    for nm in dst:
        o, sz = src[nm]
        for t in range(-(-sz // tn)):
            p_tile[o // tn + t] = dst[nm] // tn + t
    q_tile = [min(t, 3 * sb_tiles - 1) for t in range(nt)]
    tm = _tile(n, 2048)
    kern = functools.partial(_in_proj_kernel, sb_tiles=sb_tiles, last_tile=nt - 1,
                             last_valid=lay["in_w"] - (nt - 1) * tn)
    grid_spec = pltpu.PrefetchScalarGridSpec(
        num_scalar_prefetch=2,
        grid=(n // tm, nt),
        in_specs=[pl.BlockSpec((tm, d), lambda i, j, pt, qt: (i, 0), pipeline_mode=pl.Buffered(1)),
                  pl.BlockSpec((tm, LANES), lambda i, j, pt, qt: (i, 0)),
                  pl.BlockSpec((None, d, tn), lambda i, j, pt, qt: (layer, 0, j)),
                  pl.BlockSpec((1, HEAD_DIM), lambda i, j, pt, qt: (0, 0)),
                  pl.BlockSpec((1, HEAD_DIM), lambda i, j, pt, qt: (0, 0))],
        out_specs=[pl.BlockSpec((tm, tn), lambda i, j, pt, qt: (i, pt[j])),
                   pl.BlockSpec((tm, tn), lambda i, j, pt, qt: (i, qt[j]))])
    return pl.pallas_call(
        kern,
        grid_spec=grid_spec,
        out_shape=[jax.ShapeDtypeStruct((n, lay["p_w"]), F32),
                   jax.ShapeDtypeStruct((n, 3 * lay["sb_w"]), BF16)],
        compiler_params=_params("parallel", "arbitrary"),
        name="mix_in_proj",
    )(jnp.asarray(p_tile, jnp.int32), jnp.asarray(q_tile, jnp.int32), xg, ss, w_in,
      q_gain.reshape(1, HEAD_DIM), k_gain.reshape(1, HEAD_DIM))


def _out_proj_kernel(ya_ref, yb_ref, yc_ref, w_ref, x_ref, g_ref, o_ref, xg_ref, ss_ref, y_ref):
    ka, kb = ya_ref.shape[1], yb_ref.shape[1]

    @pl.when(pl.program_id(1) == 0)
    def _():
        y_ref[:, 0:ka] = ya_ref[...]
        y_ref[:, ka:ka + kb] = yb_ref[...]
        y_ref[:, ka + kb:] = yc_ref[...]

    y = x_ref[...] + jnp.dot(y_ref[...], w_ref[...], preferred_element_type=F32)
    _store_residual(y, g_ref, o_ref, xg_ref, ss_ref)


def _out_proj(ya, yb, yc, w, layer, x2, g_next):
    n, d = x2.shape
    tm = _tile(n, 1024)
    tn = _tile(d, 512)
    ka, kb, kc = ya.shape[1], yb.shape[1], yc.shape[1]
    kw = w.shape[1]
    assert ka + kb + kc == kw
    out_specs, out_shape = _residual_out(n, d, tm, tn, True)
    return pl.pallas_call(
        _out_proj_kernel,
        grid=(n // tm, d // tn),
        in_specs=[pl.BlockSpec((tm, ka), lambda i, j: (i, 0)),
                  pl.BlockSpec((tm, kb), lambda i, j: (i, 0)),
                  pl.BlockSpec((tm, kc), lambda i, j: (i, 0)),
                  pl.BlockSpec((None, kw, tn), lambda i, j: (layer, 0, j)),
                  pl.BlockSpec((tm, tn), lambda i, j: (i, j)),
                  pl.BlockSpec((1, tn), lambda i, j: (0, j))],
        out_specs=out_specs, out_shape=out_shape,
        scratch_shapes=[pltpu.VMEM((tm, kw), BF16)],
        compiler_params=_params("parallel", "arbitrary"),
        name="mix_out_proj",
    )(ya, yb, yc, w, x2, g_next.reshape(1, d))


SB_T = 256
SB_DEAD = 105.0


def _sb_attn_kernel(q_ref, k_ref, v_ref, o_ref):
    t = SB_T
    i = pl.program_id(2)
    q = q_ref[...]
    row = lax.broadcasted_iota(jnp.int32, (t, t), 0)
    col = lax.broadcasted_iota(jnp.int32, (t, t), 1)
    tri = jnp.where(row >= col, 1.0, 0.0).astype(BF16)

    def block(kb, carry, limit):
        acc, run = carry
        start = pl.multiple_of(kb * t, t)
        k = k_ref[pl.ds(start, t), :]
        v = v_ref[pl.ds(start, t), :]
        z = lax.dot_general(q, k, (((1,), (1,)), ((), ())), preferred_element_type=F32)
        sp = _softplus(z)
        if limit is not None:
            keep = col < limit
            sp = jnp.where(keep, sp, 0.0)
        hi, lo = _split_bf16(sp, 2)
        c = (jnp.dot(hi, tri, preferred_element_type=F32)
             + jnp.dot(lo, tri, preferred_element_type=F32))
        e = z - c - run
        if limit is not None:
            e = jnp.where(keep, e, NEG_BIG)
        a = jnp.exp(e)
        acc = acc + jnp.dot(a.astype(BF16), v, preferred_element_type=F32)
        return acc, run + c[:, :1]

    carry = block(i, (jnp.zeros((t, HEAD_DIM), F32), jnp.zeros((t, 1), F32)), row)
    acc, run = block(jnp.maximum(i - 1, 0), carry, jnp.where(i >= 1, t, 0))

    def live(st):
        kb, _, _, run_min = st
        return jnp.logical_and(kb >= 0, run_min < SB_DEAD)

    def step(st):
        kb, acc, run, _ = st
        acc, run = block(kb, (acc, run), None)
        return kb - 1, acc, run, jnp.min(run)

    _, acc, _, _ = lax.while_loop(live, step, (i - 2, acc, run, jnp.min(run)))
    o_ref[...] = acc.astype(o_ref.dtype)


def _sb_attention(qkv, batch, seq, heads):
    w = heads * HEAD_DIM
    q3 = qkv.reshape(batch, seq, 3 * w)
    k3 = v3 = q3
    assert seq % SB_T == 0
    out = pl.pallas_call(
        _sb_attn_kernel,
        grid=(batch, heads, seq // SB_T),
        in_specs=[pl.BlockSpec((None, SB_T, HEAD_DIM), lambda b, h, i: (b, i, h)),
                  pl.BlockSpec((None, seq, HEAD_DIM), lambda b, h, i: (b, 0, heads + h)),
                  pl.BlockSpec((None, seq, HEAD_DIM), lambda b, h, i: (b, 0, 2 * heads + h))],
        out_specs=pl.BlockSpec((None, SB_T, HEAD_DIM), lambda b, h, i: (b, i, h)),
        out_shape=jax.ShapeDtypeStruct((batch, seq, w), BF16),
        compiler_params=_params("parallel", "parallel", "arbitrary"),
        name="sb_attention",
    )(q3, k3, v3)
    return out.reshape(batch * seq, w)


CONV_TS = 256
CONV_HALO = 32


def _conv_kernel(val_ref, gate_ref, hval_ref, hgate_ref, w_ref, b_ref, lg_ref, lb_ref, o_ref, u_ref):
    ts = o_ref.shape[0]
    i = pl.program_id(1)
    halo = hval_ref[...] * _sigmoid(hgate_ref[...])
    u_ref[0:CONV_HALO, :] = jnp.where(i > 0, halo, 0.0)
    u_ref[CONV_HALO:, :] = val_ref[...] * _sigmoid(gate_ref[...])
    base = CONV_HALO - (CONV_K - 1)
    acc = jnp.zeros(o_ref.shape, F32) + b_ref[...]
    for k in range(CONV_K):
        acc = acc + w_ref[k:k + 1, :] * u_ref[base + k:base + k + ts, :]
    mu = jnp.mean(acc, axis=-1, keepdims=True)
    cen = acc - mu
    var = jnp.mean(cen * cen, axis=-1, keepdims=True)
    y = cen * lax.rsqrt(var + EPS) * lg_ref[...] + lb_ref[...]
    o_ref[...] = (y * _sigmoid(y)).astype(o_ref.dtype)


def _conv_module(p, batch, seq, val_off, gate_off, width, conv_w, conv_b, ln_g, ln_b):
    p3 = p.reshape(batch, seq, p.shape[1])
    ts = _tile(seq, CONV_TS)
    assert val_off % width == 0 and gate_off % width == 0 and ts % CONV_HALO == 0
    cv, cg = val_off // width, gate_off // width
    hpt = ts // CONV_HALO
    main = lambda c: pl.BlockSpec((None, ts, width), lambda b, i: (b, i, c))
    halo = lambda c: pl.BlockSpec((None, CONV_HALO, width),
                                  lambda b, i: (b, jnp.maximum(i * hpt - 1, 0), c))
    vec = lambda r: pl.BlockSpec((r, width), lambda b, i: (0, 0))
    out = pl.pallas_call(
        _conv_kernel,
        grid=(batch, seq // ts),
        in_specs=[main(cv), main(cg), halo(cv), halo(cg), vec(CONV_K), vec(1), vec(1), vec(1)],
        out_specs=pl.BlockSpec((None, ts, width), lambda b, i: (b, i, 0)),
        out_shape=jax.ShapeDtypeStruct((batch, seq, width), BF16),
        scratch_shapes=[pltpu.VMEM((CONV_HALO + ts, width), F32)],
        compiler_params=_params("parallel", "arbitrary"),
        name="conv_module",
    )(p3, p3, p3, p3, conv_w, conv_b.reshape(1, width), ln_g.reshape(1, width), ln_b.reshape(1, width))
    return out.reshape(batch * seq, width)


GLA_ROWS = 256
GLA_SUB = 16


def _gla_kernel(q_ref, k_ref, v_ref, r_ref, lr_ref, gw_ref, gb_ref, og_ref, o_ref, state_ref):
    c, sb = GLA_CHUNK, GLA_SUB
    rows = q_ref.shape[0]
    nb, spc = rows // sb, c // sb

    @pl.when(pl.program_id(2) == 0)
    def _():
        state_ref[...] = jnp.zeros_like(state_ref)

    lr_parts = _split_bf16(lr_ref[...], 2)
    gw_parts = _split_bf16(gw_ref[...], 2)
    logit = (jnp.dot(lr_parts[0], gw_parts[0], preferred_element_type=F32)
             + jnp.dot(lr_parts[1], gw_parts[0], preferred_element_type=F32)
             + jnp.dot(lr_parts[0], gw_parts[1], preferred_element_type=F32)) + gb_ref[...]
    g = -_softplus(-logit) * (1.0 / GLA_TAU)

    row = lax.broadcasted_iota(jnp.int32, (rows, rows), 0)
    col = lax.broadcasted_iota(jnp.int32, (rows, rows), 1)
    tril = jnp.where((row >= col) & (row // c == col // c), 1.0, 0.0).astype(BF16)
    b = sum(jnp.dot(tril, part, preferred_element_type=F32) for part in _split_bf16(g, 3))

    q = q_ref[...] * GLA_SCALE
    k = k_ref[...]
    b3, q3, k3 = (a.reshape(nb, sb, GLA_DK) for a in (b, q, k))
    r3 = (b3 - g.reshape(nb, sb, GLA_DK))[:, 0:1, :]
    q_sub = (q3 * jnp.exp(b3 - r3)).reshape(rows, GLA_DK).astype(BF16)

    ones = jnp.ones((GLA_DK, LANES), BF16)
    t_off = lax.broadcasted_iota(jnp.int32, (nb, sb, GLA_DK), 1)
    lane = lax.broadcasted_iota(jnp.int32, (rows, LANES), 1)
    ridx = lax.broadcasted_iota(jnp.int32, (rows, LANES), 0)
    sub_lane0 = (ridx % c) // sb * sb
    diag = jnp.zeros((rows, LANES), F32)
    for s in range(sb):
        decay = jnp.exp(jnp.where(t_off >= s, b3 - b3[:, s:s + 1, :], NEG_BIG))
        prod = (q3 * k3[:, s:s + 1, :] * decay).reshape(rows, GLA_DK).astype(BF16)
        tot = jnp.dot(prod, ones, preferred_element_type=F32)
        diag = jnp.where(lane == sub_lane0 + s, tot, diag)

    kidx = lax.broadcasted_iota(jnp.int32, (c, GLA_DK), 0)
    og = og_ref[...]
    for ci in range(rows // c):
        lo = ci * c
        bc, kc = b[lo:lo + c, :], k[lo:lo + c, :]
        vc = v_ref[lo:lo + c, :].astype(BF16)
        parts = [jnp.zeros((sb, c), F32)]
        for i in range(1, spc):
            r_i = r3[ci * spc + i]
            k_i = (kc * jnp.exp(jnp.where(kidx < i * sb, r_i - bc, NEG_BIG))).astype(BF16)
            parts.append(lax.dot_general(q_sub[lo + i * sb:lo + (i + 1) * sb, :], k_i,
                                         (((1,), (1,)), ((), ())), preferred_element_type=F32))
        scores = jnp.concatenate(parts, axis=0) + diag[lo:lo + c, :c]

        state = state_ref[...]
        o = jnp.dot(scores.astype(BF16), vc, preferred_element_type=F32)
        o = o + lax.dot_general((q[lo:lo + c, :] * jnp.exp(bc)).astype(BF16), state.astype(BF16),
                                (((1,), (1,)), ((), ())), preferred_element_type=F32)
        b_last = bc[c - 1:c, :]
        k_dec = (kc * jnp.exp(b_last - bc)).astype(BF16)
        state_ref[...] = jnp.exp(b_last) * state + lax.dot_general(
            vc, k_dec, (((0,), (0,)), ((), ())), preferred_element_type=F32)

        ms = jnp.mean(o * o, axis=-1, keepdims=True)
        o = o * lax.rsqrt(ms + EPS) * og
        gate = r_ref[lo:lo + c, :]
        o_ref[lo:lo + c, :] = (o * gate * _sigmoid(gate)).astype(o_ref.dtype)


def _gla(p, batch, seq, heads, q_off, k_off, v_off, r_off, lr_off, gate_w_p, gate_b, out_gain):
    p3 = p.reshape(batch, seq, p.shape[1])
    rows = _tile(seq, GLA_ROWS)
    assert rows % GLA_CHUNK == 0
    assert q_off % GLA_DK == 0 and k_off % GLA_DK == 0 and lr_off % LANES == 0
    assert v_off % GLA_DV == 0 and r_off % GLA_DV == 0
    spec = lambda width, blk0: pl.BlockSpec((None, rows, width), lambda b, h, i: (b, i, blk0 + h))
    out = pl.pallas_call(
        _gla_kernel,
        grid=(batch, heads, seq // rows),
        in_specs=[spec(GLA_DK, q_off // GLA_DK), spec(GLA_DK, k_off // GLA_DK),
                  spec(GLA_DV, v_off // GLA_DV), spec(GLA_DV, r_off // GLA_DV),
                  pl.BlockSpec((None, rows, LANES), lambda b, h, i: (b, i, lr_off // LANES)),
                  pl.BlockSpec((LANES, GLA_DK), lambda b, h, i: (0, h)),
                  pl.BlockSpec((1, GLA_DK), lambda b, h, i: (0, h)),
                  pl.BlockSpec((1, GLA_DV), lambda b, h, i: (0, 0))],
        out_specs=pl.BlockSpec((None, rows, GLA_DV), lambda b, h, i: (b, i, h)),
        out_shape=jax.ShapeDtypeStruct((batch, seq, heads * GLA_DV), BF16),
        scratch_shapes=[pltpu.VMEM((GLA_DV, GLA_DK), F32)],
        compiler_params=_params("parallel", "parallel", "arbitrary"),
        name="gla",
    )(p3, p3, p3, p3, p3, gate_w_p, gate_b.reshape(1, -1), out_gain.reshape(1, GLA_DV))
    return out.reshape(batch * seq, heads * GLA_DV)


def _layout(d_model):
    sb_w = 3 * d_model // 8
    conv_w = d_model // 4
    gla_w = d_model - sb_w - conv_w
    gla_heads = gla_w // GLA_DV
    key_w = gla_heads * GLA_DK
    names = ("sb_q", "sb_k", "sb_v", "conv_val", "conv_gate", "gla_q", "gla_k", "gla_v", "gla_r", "gla_lr")
    sizes = (sb_w, sb_w, sb_w, conv_w, conv_w, key_w, key_w, gla_w, gla_w, GLA_RANK)
    src, o = {}, 0
    for nm, sz in zip(names, sizes):
        src[nm] = (o, sz)
        o += sz
    order = ("conv_val", "conv_gate", "gla_v", "gla_r", "gla_q", "gla_k", "gla_lr")
    dst, o = {}, 0
    for nm in order:
        dst[nm] = o
        o += -(-src[nm][1] // INPROJ_TN) * INPROJ_TN
    return dict(sb_w=sb_w, conv_w=conv_w, gla_w=gla_w, sb_heads=sb_w // HEAD_DIM,
                gla_heads=gla_heads, src=src, dst=dst, p_w=o, in_w=sum(sizes))


def kernel(x, ffn1_norm, ffn1_w_in, ffn1_w_out, mix_norm, mix_w_in, sb_q_gain, sb_k_gain, conv_w, conv_b, conv_ln_g, conv_ln_b, gla_gate_w, gla_gate_b, gla_out_gain, mix_w_out, ffn2_norm, ffn2_w_in, ffn2_w_out, final_norm):
    batch, seq, d = x.shape
    depth = ffn1_norm.shape[0]
    lay = _layout(d)
    dst = lay["dst"]
    assert mix_w_in.shape[2] == lay["in_w"]

    ffn1_w_out, ffn2_w_out, mix_w_out = (w.astype(BF16) for w in (ffn1_w_out, ffn2_w_out, mix_w_out))
    x2 = x.reshape(batch * seq, d)
    xg, ss = _norm(x2, None, ffn1_norm[0])
    for l in range(depth):
        x2, xg, ss = _ffn(x2, xg, ss, ffn1_w_in, ffn1_w_out, l, mix_norm[l])

        p, qkv = _in_proj(xg, ss, mix_w_in, l, lay, sb_q_gain[l], sb_k_gain[l])
        ya = _sb_attention(qkv, batch, seq, lay["sb_heads"])
        yb = _conv_module(p, batch, seq, dst["conv_val"], dst["conv_gate"], lay["conv_w"],
                          conv_w[l], conv_b[l], conv_ln_g[l], conv_ln_b[l])
        gate_w_p = jnp.pad(gla_gate_w[l], ((0, LANES - GLA_RANK), (0, 0)))
        yc = _gla(p, batch, seq, lay["gla_heads"], dst["gla_q"], dst["gla_k"], dst["gla_v"],
                  dst["gla_r"], dst["gla_lr"], gate_w_p, gla_gate_b[l], gla_out_gain[l])
        x2, xg, ss = _out_proj(ya, yb, yc, mix_w_out, l, x2, ffn2_norm[l])

        (x2,) = _ffn(x2, xg, ss, ffn2_w_in, ffn2_w_out, l, None)
        if l + 1 < depth:
            x2, xg, ss = _norm(x2, final_norm[l], ffn1_norm[l + 1])
        else:
            (x2,) = _norm(x2, final_norm[l], None)
    return x2.reshape(batch, seq, d)
```

```python
import functools

import jax
import jax.numpy as jnp
import numpy as np
from jax import lax
from jax.experimental import pallas as pl
from jax.experimental.pallas import tpu as pltpu

F32 = jnp.float32
BF16 = jnp.bfloat16

HEAD_DIM = 128
GLA_DK = 128
GLA_DV = 256
GLA_RANK = 16
GLA_TAU = 16.0
GLA_CHUNK = 64
CONV_K = 31
EPS = 1e-6
SB_SCALE = HEAD_DIM ** -0.5
GLA_SCALE = GLA_DK ** -0.5

LANES = 128
SUBLANES = 8
VMEM_LIMIT_BYTES = 56 * 1024 * 1024

NEG_BIG = -1e30


def _params(*sem):
    return pltpu.CompilerParams(dimension_semantics=sem, vmem_limit_bytes=VMEM_LIMIT_BYTES)


def _tile(n, want):
    if n <= want:
        return n
    t = want
    while n % t:
        t -= SUBLANES
    return t


def _split_bf16(x, terms):
    out = []
    r = x
    for _ in range(terms):
        h = r.astype(BF16)
        out.append(h)
        r = r - h.astype(F32)
    return out


def _softplus(z):
    return jnp.maximum(z, 0.0) + jnp.log(1.0 + jnp.exp(-jnp.abs(z)))


def _sigmoid(z):
    return 1.0 / (1.0 + jnp.exp(-z))


def _row_scale(ss_ref, d):
    return lax.rsqrt(ss_ref[:, 0:1] * (1.0 / d) + EPS)


def _store_residual(y, g_ref, o_ref, xg_ref, ss_ref):
    o_ref[...] = y
    if xg_ref is None:
        return
    xg_ref[...] = (y * g_ref[...]).astype(xg_ref.dtype)
    part = jnp.broadcast_to(jnp.sum(y * y, axis=-1, keepdims=True), ss_ref.shape)
    j = pl.program_id(1)

    @pl.when(j == 0)
    def _():
        ss_ref[...] = part

    @pl.when(j > 0)
    def _():
        ss_ref[...] += part


def _residual_out(n, d, tm, tn, with_norm):
    specs = [pl.BlockSpec((tm, tn), lambda i, j: (i, j))]
    shapes = [jax.ShapeDtypeStruct((n, d), F32)]
    if with_norm:
        specs += [pl.BlockSpec((tm, tn), lambda i, j: (i, j)), pl.BlockSpec((tm, LANES), lambda i, j: (i, 0))]
        shapes += [jax.ShapeDtypeStruct((n, d), BF16), jax.ShapeDtypeStruct((n, LANES), F32)]
    return specs, shapes


def _norm_kernel(*refs, d, with_out, with_pair):
    refs = list(refs)
    x_ref = refs.pop(0)
    x = x_ref[...]
    if with_out:
        g_ref = refs.pop(0)
    if with_pair:
        gn_ref = refs.pop(0)
    if with_out:
        o_ref = refs.pop(0)
        ms = jnp.mean(x * x, axis=-1, keepdims=True)
        x = x * lax.rsqrt(ms + EPS) * g_ref[...]
        o_ref[...] = x
    if with_pair:
        xg_ref, ss_ref = refs
        xg_ref[...] = (x * gn_ref[...]).astype(xg_ref.dtype)
        ss_ref[...] = jnp.broadcast_to(jnp.sum(x * x, axis=-1, keepdims=True), ss_ref.shape)


def _norm(x2, g_out, g_pair):
    n, d = x2.shape
    tm = _tile(n, 256)
    row = pl.BlockSpec((tm, d), lambda i: (i, 0))
    vec = pl.BlockSpec((1, d), lambda i: (0, 0))
    ins, in_specs, out_specs, out_shape = [x2], [row], [], []
    if g_out is not None:
        ins.append(g_out.reshape(1, d)); in_specs.append(vec)
        out_specs.append(row); out_shape.append(jax.ShapeDtypeStruct((n, d), F32))
    if g_pair is not None:
        ins.append(g_pair.reshape(1, d)); in_specs.append(vec)
        out_specs += [row, pl.BlockSpec((tm, LANES), lambda i: (i, 0))]
        out_shape += [jax.ShapeDtypeStruct((n, d), BF16), jax.ShapeDtypeStruct((n, LANES), F32)]
    return pl.pallas_call(
        functools.partial(_norm_kernel, d=d, with_out=g_out is not None, with_pair=g_pair is not None),
        grid=(n // tm,),
        in_specs=in_specs, out_specs=out_specs, out_shape=out_shape,
        compiler_params=_params("parallel"),
        name="rmsnorm",
    )(*ins)


FFN_TF = 256


def _ffn_up_kernel(xg_ref, ss_ref, wg_ref, wu_ref, o_ref):
    xg = xg_ref[...]
    s = _row_scale(ss_ref, xg.shape[1])
    g = s * jnp.dot(xg, wg_ref[...].astype(BF16), preferred_element_type=F32)
    u = s * jnp.dot(xg, wu_ref[...].astype(BF16), preferred_element_type=F32)
    o_ref[...] = (g * _sigmoid(g) * u).astype(o_ref.dtype)


def _ffn_down_kernel(a_ref, w_ref, x_ref, *rest):
    g_ref, o_ref, xg_ref, ss_ref = rest if len(rest) == 4 else (None, rest[0], None, None)
    acc = jnp.dot(a_ref[...], w_ref[...], preferred_element_type=F32)
    _store_residual(x_ref[...] + 0.5 * acc, g_ref, o_ref, xg_ref, ss_ref)


def _ffn(x2, xg, ss, w_in, w_out, layer, g_next):
    n, d = x2.shape
    f = w_out.shape[1]
    assert w_in.shape[1:] == (d, 2 * f) and f % FFN_TF == 0
    nj = f // FFN_TF
    tm = _tile(n, 2048)
    a = pl.pallas_call(
        _ffn_up_kernel,
        grid=(n // tm, nj),
        in_specs=[pl.BlockSpec((tm, d), lambda i, j: (i, 0), pipeline_mode=pl.Buffered(1)),
                  pl.BlockSpec((tm, LANES), lambda i, j: (i, 0)),
                  pl.BlockSpec((None, d, FFN_TF), lambda i, j: (layer, 0, j)),
                  pl.BlockSpec((None, d, FFN_TF), lambda i, j: (layer, 0, nj + j))],
        out_specs=pl.BlockSpec((tm, FFN_TF), lambda i, j: (i, j)),
        out_shape=jax.ShapeDtypeStruct((n, f), BF16),
        compiler_params=_params("parallel", "arbitrary"),
        name="ffn_up",
    )(xg, ss, w_in, w_in)
    tm2 = _tile(n, 512)
    tn = _tile(d, 512)
    with_norm = g_next is not None
    out_specs, out_shape = _residual_out(n, d, tm2, tn, with_norm)
    ins = [a, w_out, x2] + ([g_next.reshape(1, d)] if with_norm else [])
    in_specs = [pl.BlockSpec((tm2, f), lambda i, j: (i, 0)),
                pl.BlockSpec((None, f, tn), lambda i, j: (layer, 0, j)),
                pl.BlockSpec((tm2, tn), lambda i, j: (i, j))]
    if with_norm:
        in_specs.append(pl.BlockSpec((1, tn), lambda i, j: (0, j)))
    return pl.pallas_call(
        _ffn_down_kernel,
        grid=(n // tm2, d // tn),
        in_specs=in_specs, out_specs=out_specs, out_shape=out_shape,
        compiler_params=_params("parallel", "arbitrary"),
        name="ffn_down",
    )(*ins)


INPROJ_TN = 512


def _in_proj_kernel(pt_ref, qt_ref, xg_ref, ss_ref, w_ref, qg_ref, kg_ref, p_ref, qkv_ref, *,
                    sb_tiles, last_tile, last_valid):
    del pt_ref, qt_ref
    j = pl.program_id(1)
    xg = xg_ref[...]
    acc = _row_scale(ss_ref, xg.shape[1]) * jnp.dot(xg, w_ref[...], preferred_element_type=F32)

    def head_norm(gain):
        for c0 in range(0, INPROJ_TN, HEAD_DIM):
            x = acc[:, c0:c0 + HEAD_DIM]
            ms = jnp.mean(x * x, axis=-1, keepdims=True)
            qkv_ref[:, c0:c0 + HEAD_DIM] = (x * lax.rsqrt(ms + EPS) * gain).astype(qkv_ref.dtype)

    @pl.when(j < sb_tiles)
    def _():
        head_norm(qg_ref[...] * SB_SCALE)

    @pl.when(jnp.logical_and(j >= sb_tiles, j < 2 * sb_tiles))
    def _():
        head_norm(kg_ref[...])

    @pl.when(jnp.logical_and(j >= 2 * sb_tiles, j < 3 * sb_tiles))
    def _():
        qkv_ref[...] = acc.astype(qkv_ref.dtype)

    @pl.when(jnp.logical_and(j >= 3 * sb_tiles, j < last_tile))
    def _():
        p_ref[...] = acc

    @pl.when(j == last_tile)
    def _():
        lane = lax.broadcasted_iota(jnp.int32, acc.shape, 1)
        p_ref[...] = jnp.where(lane < last_valid, acc, 0.0)


def _in_proj(xg, ss, w_in, layer, lay, q_gain, k_gain):
    n, d = xg.shape
    src, dst = lay["src"], lay["dst"]
    tn = INPROJ_TN
    nt = -(-lay["in_w"] // tn)
    sb_tiles = lay["sb_w"] // tn
    assert lay["sb_w"] % tn == 0 and src["sb_q"][0] == 0 and src["sb_k"][0] == lay["sb_w"]
    col_dst = np.full(nt * tn, -1)
    for nm in dst:
        o, sz = src[nm]
        col_dst[o:o + sz] = dst[nm] + np.arange(sz)
    p_tile = []
    for t in range(nt):
        cols = col_dst[t * tn:(t + 1) * tn]
        real = np.flatnonzero(cols >= 0)
        first = int(cols[real[0]] - real[0]) if real.size else 0
        assert first % tn == 0 and (cols[real] == first + real).all() and real.size in (0, tn, lay["in_w"] % tn)
        p_tile.append(first // tn)
    q_tile = [min(t, 3 * sb_tiles - 1) for t in range(nt)]
    tm = _tile(n, 2048)
    kern = functools.partial(_in_proj_kernel, sb_tiles=sb_tiles, last_tile=nt - 1,
                             last_valid=lay["in_w"] - (nt - 1) * tn)
    grid_spec = pltpu.PrefetchScalarGridSpec(
        num_scalar_prefetch=2,
        grid=(n // tm, nt),
        in_specs=[pl.BlockSpec((tm, d), lambda i, j, pt, qt: (i, 0), pipeline_mode=pl.Buffered(1)),
                  pl.BlockSpec((tm, LANES), lambda i, j, pt, qt: (i, 0)),
                  pl.BlockSpec((None, d, tn), lambda i, j, pt, qt: (layer, 0, j)),
                  pl.BlockSpec((1, HEAD_DIM), lambda i, j, pt, qt: (0, 0)),
                  pl.BlockSpec((1, HEAD_DIM), lambda i, j, pt, qt: (0, 0))],
        out_specs=[pl.BlockSpec((tm, tn), lambda i, j, pt, qt: (i, pt[j])),
                   pl.BlockSpec((tm, tn), lambda i, j, pt, qt: (i, qt[j]))])
    return pl.pallas_call(
        kern,
        grid_spec=grid_spec,
        out_shape=[jax.ShapeDtypeStruct((n, lay["p_w"]), F32),
                   jax.ShapeDtypeStruct((n, 3 * lay["sb_w"]), BF16)],
        compiler_params=_params("parallel", "arbitrary"),
        name="mix_in_proj",
    )(jnp.asarray(p_tile, jnp.int32), jnp.asarray(q_tile, jnp.int32), xg, ss, w_in,
      q_gain.reshape(1, HEAD_DIM), k_gain.reshape(1, HEAD_DIM))


def _out_proj_kernel(ya_ref, yb_ref, yc_ref, w_ref, x_ref, g_ref, o_ref, xg_ref, ss_ref, y_ref):
    ka, kb = ya_ref.shape[1], yb_ref.shape[1]

    @pl.when(pl.program_id(1) == 0)
    def _():
        y_ref[:, 0:ka] = ya_ref[...]
        y_ref[:, ka:ka + kb] = yb_ref[...]
        y_ref[:, ka + kb:] = yc_ref[...]

    y = x_ref[...] + jnp.dot(y_ref[...], w_ref[...], preferred_element_type=F32)
    _store_residual(y, g_ref, o_ref, xg_ref, ss_ref)


def _out_proj(ya, yb, yc, w, layer, x2, g_next):
    n, d = x2.shape
    tm = _tile(n, 1024)
    tn = _tile(d, 512)
    ka, kb, kc = ya.shape[1], yb.shape[1], yc.shape[1]
    kw = w.shape[1]
    assert ka + kb + kc == kw
    out_specs, out_shape = _residual_out(n, d, tm, tn, True)
    return pl.pallas_call(
        _out_proj_kernel,
        grid=(n // tm, d // tn),
        in_specs=[pl.BlockSpec((tm, ka), lambda i, j: (i, 0)),
                  pl.BlockSpec((tm, kb), lambda i, j: (i, 0)),
                  pl.BlockSpec((tm, kc), lambda i, j: (i, 0)),
                  pl.BlockSpec((None, kw, tn), lambda i, j: (layer, 0, j)),
                  pl.BlockSpec((tm, tn), lambda i, j: (i, j)),
                  pl.BlockSpec((1, tn), lambda i, j: (0, j))],
        out_specs=out_specs, out_shape=out_shape,
        scratch_shapes=[pltpu.VMEM((tm, kw), BF16)],
        compiler_params=_params("parallel", "arbitrary"),
        name="mix_out_proj",
    )(ya, yb, yc, w, x2, g_next.reshape(1, d))


SB_T = 256
SB_HEADS_PER_STEP = 2
SB_DEAD = 105.0


def _sb_attn_kernel(q_ref, k_ref, v_ref, o_ref):
    t = SB_T
    i = pl.program_id(2)
    heads = range(q_ref.shape[1] // HEAD_DIM)
    row = lax.broadcasted_iota(jnp.int32, (t, t), 0)
    col = lax.broadcasted_iota(jnp.int32, (t, t), 1)
    tri = jnp.where(row >= col, 1.0, 0.0).astype(BF16)

    def block(h, kb, carry, limit):
        acc, run = carry
        cols = slice(h * HEAD_DIM, (h + 1) * HEAD_DIM)
        start = pl.multiple_of(kb * t, t)
        k = k_ref[pl.ds(start, t), cols]
        v = v_ref[pl.ds(start, t), cols]
        z = lax.dot_general(q_ref[:, cols], k, (((1,), (1,)), ((), ())), preferred_element_type=F32)
        sp = _softplus(z)
        if limit is not None:
            keep = col < limit
            sp = jnp.where(keep, sp, 0.0)
        hi, lo = _split_bf16(sp, 2)
        c = (jnp.dot(hi, tri, preferred_element_type=F32)
             + jnp.dot(lo, tri, preferred_element_type=F32))
        e = z - c - run
        if limit is not None:
            e = jnp.where(keep, e, NEG_BIG)
        a = jnp.exp(e)
        acc = acc + jnp.dot(a.astype(BF16), v, preferred_element_type=F32)
        return acc, run + c[:, :1]

    carries = [(jnp.zeros((t, HEAD_DIM), F32), jnp.zeros((t, 1), F32)) for _ in heads]
    carries = [block(h, i, carries[h], row) for h in heads]
    carries = [block(h, jnp.maximum(i - 1, 0), carries[h], jnp.where(i >= 1, t, 0)) for h in heads]

    def run_min(carries):
        return functools.reduce(jnp.minimum, [jnp.min(run) for _, run in carries])

    def live(st):
        kb, _, low = st
        return jnp.logical_and(kb >= 0, low < SB_DEAD)

    def step(st):
        kb, carries, _ = st
        carries = [block(h, kb, carries[h], None) for h in heads]
        return kb - 1, carries, run_min(carries)

    _, carries, _ = lax.while_loop(live, step, (i - 2, carries, run_min(carries)))
    for h in heads:
        o_ref[:, h * HEAD_DIM:(h + 1) * HEAD_DIM] = carries[h][0].astype(o_ref.dtype)


def _sb_attention(qkv, batch, seq, heads):
    w = heads * HEAD_DIM
    q3 = qkv.reshape(batch, seq, 3 * w)
    k3 = v3 = q3
    hps = SB_HEADS_PER_STEP if heads % SB_HEADS_PER_STEP == 0 else 1
    groups, gw = heads // hps, hps * HEAD_DIM
    assert seq % SB_T == 0
    out = pl.pallas_call(
        _sb_attn_kernel,
        grid=(batch, groups, seq // SB_T),
        in_specs=[pl.BlockSpec((None, SB_T, gw), lambda b, g, i: (b, i, g)),
                  pl.BlockSpec((None, seq, gw), lambda b, g, i: (b, 0, groups + g)),
                  pl.BlockSpec((None, seq, gw), lambda b, g, i: (b, 0, 2 * groups + g))],
        out_specs=pl.BlockSpec((None, SB_T, gw), lambda b, g, i: (b, i, g)),
        out_shape=jax.ShapeDtypeStruct((batch, seq, w), BF16),
        compiler_params=_params("parallel", "parallel", "arbitrary"),
        name="sb_attention",
    )(q3, k3, v3)
    return out.reshape(batch * seq, w)


CONV_TS = 256
CONV_HALO = 32
CONV_ROWS = 32


def _conv_kernel(val_ref, gate_ref, hval_ref, hgate_ref, w_ref, b_ref, lg_ref, lb_ref, o_ref,
                 u_ref, us_ref):
    ts = o_ref.shape[0]
    i = pl.program_id(1)
    halo = hval_ref[...] * _sigmoid(hgate_ref[...])
    u_ref[0:CONV_HALO, :] = jnp.where(i > 0, halo, 0.0)
    u_ref[CONV_HALO:, :] = val_ref[...] * _sigmoid(gate_ref[...])
    span = CONV_HALO + ts - SUBLANES
    for m in range(1, SUBLANES):
        us_ref[m - 1, 0:span, :] = u_ref[m:m + span, :]
    base = CONV_HALO - (CONV_K - 1)
    bias, lg, lb = b_ref[...], lg_ref[...], lb_ref[...]
    width = o_ref.shape[1]
    for r0 in range(0, ts, CONV_ROWS):
        acc = jnp.zeros((CONV_ROWS // SUBLANES, SUBLANES, width), F32)
        for k in range(CONV_K):
            vreg_row, m = divmod(base + k, SUBLANES)
            lo = vreg_row * SUBLANES + r0
            src = u_ref[lo:lo + CONV_ROWS, :] if m == 0 else us_ref[m - 1, lo:lo + CONV_ROWS, :]
            acc = acc + w_ref[k] * src.reshape(acc.shape)
        acc = acc.reshape(CONV_ROWS, width) + bias
        mu = jnp.mean(acc, axis=-1, keepdims=True)
        cen = acc - mu
        var = jnp.mean(cen * cen, axis=-1, keepdims=True)
        y = cen * lax.rsqrt(var + EPS) * lg + lb
        o_ref[r0:r0 + CONV_ROWS, :] = (y * _sigmoid(y)).astype(o_ref.dtype)


def _conv_module(p, batch, seq, val_off, gate_off, width, conv_w, conv_b, ln_g, ln_b):
    p3 = p.reshape(batch, seq, p.shape[1])
    ts = _tile(seq, CONV_TS)
    assert val_off % width == 0 and gate_off % width == 0 and ts % CONV_HALO == 0
    cv, cg = val_off // width, gate_off // width
    hpt = ts // CONV_HALO
    main = lambda c: pl.BlockSpec((None, ts, width), lambda b, i: (b, i, c))
    halo = lambda c: pl.BlockSpec((None, CONV_HALO, width),
                                  lambda b, i: (b, jnp.maximum(i * hpt - 1, 0), c))
    vec = lambda r: pl.BlockSpec((r, width), lambda b, i: (0, 0))
    out = pl.pallas_call(
        _conv_kernel,
        grid=(batch, seq // ts),
        in_specs=[main(cv), main(cg), halo(cv), halo(cg),
                  pl.BlockSpec((CONV_K, SUBLANES, width), lambda b, i: (0, 0, 0)), vec(1), vec(1), vec(1)],
        out_specs=pl.BlockSpec((None, ts, width), lambda b, i: (b, i, 0)),
        out_shape=jax.ShapeDtypeStruct((batch, seq, width), BF16),
        scratch_shapes=[pltpu.VMEM((CONV_HALO + ts, width), F32),
                        pltpu.VMEM((SUBLANES - 1, CONV_HALO + ts, width), F32)],
        compiler_params=_params("parallel", "arbitrary"),
        name="conv_module",
    )(p3, p3, p3, p3, jnp.broadcast_to(conv_w[:, None, :], (CONV_K, SUBLANES, width)),
      conv_b.reshape(1, width), ln_g.reshape(1, width), ln_b.reshape(1, width))
    return out.reshape(batch * seq, width)


GLA_ROWS = 256
GLA_SUB = 16


def _gla_kernel(q_ref, k_ref, v_ref, r_ref, lr_ref, gw_ref, gb_ref, og_ref, o_ref, state_ref):
    c, sb = GLA_CHUNK, GLA_SUB
    rows = q_ref.shape[0]
    nb, spc = rows // sb, c // sb

    @pl.when(pl.program_id(2) == 0)
    def _():
        state_ref[...] = jnp.zeros_like(state_ref)

    lr_parts = _split_bf16(lr_ref[...], 2)
    gw_parts = _split_bf16(gw_ref[...], 2)
    logit = (jnp.dot(lr_parts[0], gw_parts[0], preferred_element_type=F32)
             + jnp.dot(lr_parts[1], gw_parts[0], preferred_element_type=F32)
             + jnp.dot(lr_parts[0], gw_parts[1], preferred_element_type=F32)) + gb_ref[...]
    g = -_softplus(-logit) * (1.0 / GLA_TAU)

    row = lax.broadcasted_iota(jnp.int32, (rows, rows), 0)
    col = lax.broadcasted_iota(jnp.int32, (rows, rows), 1)
    tril = jnp.where((row >= col) & (row // c == col // c), 1.0, 0.0).astype(BF16)
    b = sum(jnp.dot(tril, part, preferred_element_type=F32) for part in _split_bf16(g, 3))

    q = q_ref[...] * GLA_SCALE
    k = k_ref[...]
    b3, q3, k3 = (a.reshape(nb, sb, GLA_DK) for a in (b, q, k))
    r3 = (b3 - g.reshape(nb, sb, GLA_DK))[:, 0:1, :]
    q_sub = (q3 * jnp.exp(b3 - r3)).reshape(rows, GLA_DK).astype(BF16)

    ones = jnp.ones((GLA_DK, LANES), BF16)
    half = sb // 2
    t_half = lax.broadcasted_iota(jnp.int32, (nb, half, GLA_DK), 1)
    lane = lax.broadcasted_iota(jnp.int32, (nb, half, LANES), 2)
    sub = lax.broadcasted_iota(jnp.int32, (nb, half, LANES), 0) % (c // sb)
    off_lo = off_hi = lane - sub * sb
    halves = lambda a: (a[:, :half, :], a[:, half:, :])
    (b_lo, b_hi), (q_lo, q_hi) = halves(b3), halves(q3)

    def row_sums(qh, bh, s, first_row):
        e = bh - b3[:, s:s + 1, :]
        if s > first_row:
            e = jnp.where(t_half >= s - first_row, e, NEG_BIG)
        prod = (qh * k3[:, s:s + 1, :] * jnp.exp(e)).reshape(nb * half, GLA_DK).astype(BF16)
        return jnp.dot(prod, ones, preferred_element_type=F32).reshape(nb, half, LANES)

    diag_lo = jnp.zeros((nb, half, LANES), F32)
    diag_hi = jnp.zeros((nb, half, LANES), F32)
    for s in range(sb):
        if s < half:
            diag_lo = jnp.where(off_lo == s, row_sums(q_lo, b_lo, s, 0), diag_lo)
        diag_hi = jnp.where(off_hi == s, row_sums(q_hi, b_hi, s, half), diag_hi)
    diag = jnp.concatenate([diag_lo, diag_hi], axis=1).reshape(rows, LANES)

    kidx = lax.broadcasted_iota(jnp.int32, (c, GLA_DK), 0)
    og = og_ref[...]
    for ci in range(rows // c):
        lo = ci * c
        bc, kc = b[lo:lo + c, :], k[lo:lo + c, :]
        vc = v_ref[lo:lo + c, :].astype(BF16)
        parts = [jnp.zeros((sb, c), F32)]
        for i in range(1, spc):
            r_i = r3[ci * spc + i]
            k_i = (kc * jnp.exp(jnp.where(kidx < i * sb, r_i - bc, NEG_BIG))).astype(BF16)
            parts.append(lax.dot_general(q_sub[lo + i * sb:lo + (i + 1) * sb, :], k_i,
                                         (((1,), (1,)), ((), ())), preferred_element_type=F32))
        scores = jnp.concatenate(parts, axis=0) + diag[lo:lo + c, :c]

        state = state_ref[...]
        o = jnp.dot(scores.astype(BF16), vc, preferred_element_type=F32)
        o = o + lax.dot_general((q[lo:lo + c, :] * jnp.exp(bc)).astype(BF16), state.astype(BF16),
                                (((1,), (1,)), ((), ())), preferred_element_type=F32)
        b_last = bc[c - 1:c, :]
        k_dec = (kc * jnp.exp(b_last - bc)).astype(BF16)
        state_ref[...] = jnp.exp(b_last) * state + lax.dot_general(
            vc, k_dec, (((0,), (0,)), ((), ())), preferred_element_type=F32)

        ms = jnp.mean(o * o, axis=-1, keepdims=True)
        o = o * lax.rsqrt(ms + EPS) * og
        gate = r_ref[lo:lo + c, :]
        o_ref[lo:lo + c, :] = (o * gate * _sigmoid(gate)).astype(o_ref.dtype)


def _gla(p, batch, seq, heads, q_off, k_off, v_off, r_off, lr_off, gate_w_p, gate_b, out_gain):
    p3 = p.reshape(batch, seq, p.shape[1])
    rows = _tile(seq, GLA_ROWS)
    assert rows % GLA_CHUNK == 0
    assert q_off % GLA_DK == 0 and k_off % GLA_DK == 0 and lr_off % LANES == 0
    assert v_off % GLA_DV == 0 and r_off % GLA_DV == 0
    spec = lambda width, blk0: pl.BlockSpec((None, rows, width), lambda b, h, i: (b, i, blk0 + h))
    out = pl.pallas_call(
        _gla_kernel,
        grid=(batch, heads, seq // rows),
        in_specs=[spec(GLA_DK, q_off // GLA_DK), spec(GLA_DK, k_off // GLA_DK),
                  spec(GLA_DV, v_off // GLA_DV), spec(GLA_DV, r_off // GLA_DV),
                  pl.BlockSpec((None, rows, LANES), lambda b, h, i: (b, i, lr_off // LANES)),
                  pl.BlockSpec((LANES, GLA_DK), lambda b, h, i: (0, h)),
                  pl.BlockSpec((1, GLA_DK), lambda b, h, i: (0, h)),
                  pl.BlockSpec((1, GLA_DV), lambda b, h, i: (0, 0))],
        out_specs=pl.BlockSpec((None, rows, GLA_DV), lambda b, h, i: (b, i, h)),
        out_shape=jax.ShapeDtypeStruct((batch, seq, heads * GLA_DV), BF16),
        scratch_shapes=[pltpu.VMEM((GLA_DV, GLA_DK), F32)],
        compiler_params=_params("parallel", "parallel", "arbitrary"),
        name="gla",
    )(p3, p3, p3, p3, p3, gate_w_p, gate_b.reshape(1, -1), out_gain.reshape(1, GLA_DV))
    return out.reshape(batch * seq, heads * GLA_DV)


def _layout(d_model):
    sb_w = 3 * d_model // 8
    conv_w = d_model // 4
    gla_w = d_model - sb_w - conv_w
    gla_heads = gla_w // GLA_DV
    key_w = gla_heads * GLA_DK
    names = ("sb_q", "sb_k", "sb_v", "conv_val", "conv_gate", "gla_q", "gla_k", "gla_v", "gla_r", "gla_lr")
    sizes = (sb_w, sb_w, sb_w, conv_w, conv_w, key_w, key_w, gla_w, gla_w, GLA_RANK)
    src, o = {}, 0
    for nm, sz in zip(names, sizes):
        src[nm] = (o, sz)
        o += sz
    order = ("conv_val", "conv_gate", "gla_v", "gla_r", "gla_q", "gla_k", "gla_lr")
    dst, o = {}, 0
    for nm in order:
        dst[nm] = o
        o += src[nm][1]
    o = dst["gla_lr"] + INPROJ_TN
    return dict(sb_w=sb_w, conv_w=conv_w, gla_w=gla_w, sb_heads=sb_w // HEAD_DIM,
                gla_heads=gla_heads, src=src, dst=dst, p_w=o, in_w=sum(sizes))


def kernel(x, ffn1_norm, ffn1_w_in, ffn1_w_out, mix_norm, mix_w_in, sb_q_gain, sb_k_gain, conv_w, conv_b, conv_ln_g, conv_ln_b, gla_gate_w, gla_gate_b, gla_out_gain, mix_w_out, ffn2_norm, ffn2_w_in, ffn2_w_out, final_norm):
    batch, seq, d = x.shape
    depth = ffn1_norm.shape[0]
    lay = _layout(d)
    dst = lay["dst"]
    assert mix_w_in.shape[2] == lay["in_w"]

    ffn1_w_out, ffn2_w_out, mix_w_out, mix_w_in = (
        w.astype(BF16) for w in (ffn1_w_out, ffn2_w_out, mix_w_out, mix_w_in))
    x2 = x.reshape(batch * seq, d)
    xg, ss = _norm(x2, None, ffn1_norm[0])
    for l in range(depth):
        x2, xg, ss = _ffn(x2, xg, ss, ffn1_w_in, ffn1_w_out, l, mix_norm[l])

        p, qkv = _in_proj(xg, ss, mix_w_in, l, lay, sb_q_gain[l], sb_k_gain[l])
        ya = _sb_attention(qkv, batch, seq, lay["sb_heads"])
        yb = _conv_module(p, batch, seq, dst["conv_val"], dst["conv_gate"], lay["conv_w"],
                          conv_w[l], conv_b[l], conv_ln_g[l], conv_ln_b[l])
        gate_w_p = jnp.pad(gla_gate_w[l], ((0, LANES - GLA_RANK), (0, 0)))
        yc = _gla(p, batch, seq, lay["gla_heads"], dst["gla_q"], dst["gla_k"], dst["gla_v"],
                  dst["gla_r"], dst["gla_lr"], gate_w_p, gla_gate_b[l], gla_out_gain[l])
        x2, xg, ss = _out_proj(ya, yb, yc, mix_w_out, l, x2, ffn2_norm[l])

        (x2,) = _ffn(x2, xg, ss, ffn2_w_in, ffn2_w_out, l, None)
        if l + 1 < depth:
            x2, xg, ss = _norm(x2, final_norm[l], ffn1_norm[l + 1])
        else:
            (x2,) = _norm(x2, final_norm[l], None)
    return x2.reshape(batch, seq, d)
```

```python
import functools

import jax
import jax.numpy as jnp
import numpy as np
from jax import lax
from jax.experimental import pallas as pl
from jax.experimental.pallas import tpu as pltpu

F32 = jnp.float32
BF16 = jnp.bfloat16

HEAD_DIM = 128
GLA_DK = 128
GLA_DV = 256
GLA_RANK = 16
GLA_TAU = 16.0
GLA_CHUNK = 64
CONV_K = 31
EPS = 1e-6
SB_SCALE = HEAD_DIM ** -0.5
GLA_SCALE = GLA_DK ** -0.5

LANES = 128
SUBLANES = 8
VMEM_LIMIT_BYTES = 56 * 1024 * 1024

NEG_BIG = -1e30


def _params(*sem):
    return pltpu.CompilerParams(dimension_semantics=sem, vmem_limit_bytes=VMEM_LIMIT_BYTES)


def _tile(n, want):
    if n <= want:
        return n
    t = want
    while n % t:
        t -= SUBLANES
    return t


def _split_bf16(x, terms):
    out = []
    r = x
    for _ in range(terms):
        h = r.astype(BF16)
        out.append(h)
        r = r - h.astype(F32)
    return out


def _softplus(z):
    return jnp.maximum(z, 0.0) + jnp.log(1.0 + jnp.exp(-jnp.abs(z)))


def _sigmoid(z):
    return 1.0 / (1.0 + jnp.exp(-z))


def _row_scale(ss_ref, d):
    return lax.rsqrt(ss_ref[:, 0:1] * (1.0 / d) + EPS)


def _store_residual(y, g_ref, o_ref, xg_ref, ss_ref):
    o_ref[...] = y
    if xg_ref is None:
        return
    xg_ref[...] = (y * g_ref[...]).astype(xg_ref.dtype)
    part = jnp.broadcast_to(jnp.sum(y * y, axis=-1, keepdims=True), ss_ref.shape)
    j = pl.program_id(1)

    @pl.when(j == 0)
    def _():
        ss_ref[...] = part

    @pl.when(j > 0)
    def _():
        ss_ref[...] += part


def _residual_out(n, d, tm, tn, with_norm):
    specs = [pl.BlockSpec((tm, tn), lambda i, j: (i, j))]
    shapes = [jax.ShapeDtypeStruct((n, d), F32)]
    if with_norm:
        specs += [pl.BlockSpec((tm, tn), lambda i, j: (i, j)), pl.BlockSpec((tm, LANES), lambda i, j: (i, 0))]
        shapes += [jax.ShapeDtypeStruct((n, d), BF16), jax.ShapeDtypeStruct((n, LANES), F32)]
    return specs, shapes


def _norm_kernel(*refs, d, with_out, with_pair):
    refs = list(refs)
    x_ref = refs.pop(0)
    x = x_ref[...]
    if with_out:
        g_ref = refs.pop(0)
    if with_pair:
        gn_ref = refs.pop(0)
    if with_out:
        o_ref = refs.pop(0)
        ms = jnp.mean(x * x, axis=-1, keepdims=True)
        x = x * lax.rsqrt(ms + EPS) * g_ref[...]
        o_ref[...] = x
    if with_pair:
        xg_ref, ss_ref = refs
        xg_ref[...] = (x * gn_ref[...]).astype(xg_ref.dtype)
        ss_ref[...] = jnp.broadcast_to(jnp.sum(x * x, axis=-1, keepdims=True), ss_ref.shape)


def _norm(x2, g_out, g_pair):
    n, d = x2.shape
    tm = _tile(n, 256)
    row = pl.BlockSpec((tm, d), lambda i: (i, 0))
    vec = pl.BlockSpec((1, d), lambda i: (0, 0))
    ins, in_specs, out_specs, out_shape = [x2], [row], [], []
    if g_out is not None:
        ins.append(g_out.reshape(1, d)); in_specs.append(vec)
        out_specs.append(row); out_shape.append(jax.ShapeDtypeStruct((n, d), F32))
    if g_pair is not None:
        ins.append(g_pair.reshape(1, d)); in_specs.append(vec)
        out_specs += [row, pl.BlockSpec((tm, LANES), lambda i: (i, 0))]
        out_shape += [jax.ShapeDtypeStruct((n, d), BF16), jax.ShapeDtypeStruct((n, LANES), F32)]
    return pl.pallas_call(
        functools.partial(_norm_kernel, d=d, with_out=g_out is not None, with_pair=g_pair is not None),
        grid=(n // tm,),
        in_specs=in_specs, out_specs=out_specs, out_shape=out_shape,
        compiler_params=_params("parallel"),
        name="rmsnorm",
    )(*ins)


FFN_TF = 256


def _ffn_up_kernel(xg_ref, ss_ref, wg_ref, wu_ref, o_ref):
    xg = xg_ref[...]
    s = _row_scale(ss_ref, xg.shape[1])
    g = s * jnp.dot(xg, wg_ref[...].astype(BF16), preferred_element_type=F32)
    u = s * jnp.dot(xg, wu_ref[...].astype(BF16), preferred_element_type=F32)
    o_ref[...] = (g * _sigmoid(g) * u).astype(o_ref.dtype)


def _ffn_down_kernel(a_ref, w_ref, x_ref, *rest):
    g_ref, o_ref, xg_ref, ss_ref = rest if len(rest) == 4 else (None, rest[0], None, None)
    acc = jnp.dot(a_ref[...], w_ref[...], preferred_element_type=F32)
    _store_residual(x_ref[...] + 0.5 * acc, g_ref, o_ref, xg_ref, ss_ref)


def _ffn(x2, xg, ss, w_in, w_out, layer, g_next):
    n, d = x2.shape
    f = w_out.shape[1]
    assert w_in.shape[1:] == (d, 2 * f) and f % FFN_TF == 0
    nj = f // FFN_TF
    tm = _tile(n, 2048)
    a = pl.pallas_call(
        _ffn_up_kernel,
        grid=(n // tm, nj),
        in_specs=[pl.BlockSpec((tm, d), lambda i, j: (i, 0), pipeline_mode=pl.Buffered(1)),
                  pl.BlockSpec((tm, LANES), lambda i, j: (i, 0)),
                  pl.BlockSpec((None, d, FFN_TF), lambda i, j: (layer, 0, j)),
                  pl.BlockSpec((None, d, FFN_TF), lambda i, j: (layer, 0, nj + j))],
        out_specs=pl.BlockSpec((tm, FFN_TF), lambda i, j: (i, j)),
        out_shape=jax.ShapeDtypeStruct((n, f), BF16),
        compiler_params=_params("parallel", "arbitrary"),
        name="ffn_up",
    )(xg, ss, w_in, w_in)
    tm2 = _tile(n, 512)
    tn = _tile(d, 512)
    with_norm = g_next is not None
    out_specs, out_shape = _residual_out(n, d, tm2, tn, with_norm)
    ins = [a, w_out, x2] + ([g_next.reshape(1, d)] if with_norm else [])
    in_specs = [pl.BlockSpec((tm2, f), lambda i, j: (i, 0)),
                pl.BlockSpec((None, f, tn), lambda i, j: (layer, 0, j)),
                pl.BlockSpec((tm2, tn), lambda i, j: (i, j))]
    if with_norm:
        in_specs.append(pl.BlockSpec((1, tn), lambda i, j: (0, j)))
    return pl.pallas_call(
        _ffn_down_kernel,
        grid=(n // tm2, d // tn),
        in_specs=in_specs, out_specs=out_specs, out_shape=out_shape,
        compiler_params=_params("parallel", "arbitrary"),
        name="ffn_down",
    )(*ins)


INPROJ_TN = 512


def _in_proj_kernel(pt_ref, qt_ref, xg_ref, ss_ref, w_ref, qg_ref, kg_ref, p_ref, qkv_ref, *,
                    sb_tiles, last_tile, last_valid):
    del pt_ref, qt_ref
    j = pl.program_id(1)
    xg = xg_ref[...]
    acc = _row_scale(ss_ref, xg.shape[1]) * jnp.dot(xg, w_ref[...], preferred_element_type=F32)

    def head_norm(gain):
        for c0 in range(0, INPROJ_TN, HEAD_DIM):
            x = acc[:, c0:c0 + HEAD_DIM]
            ms = jnp.mean(x * x, axis=-1, keepdims=True)
            qkv_ref[:, c0:c0 + HEAD_DIM] = (x * lax.rsqrt(ms + EPS) * gain).astype(qkv_ref.dtype)

    @pl.when(j < sb_tiles)
    def _():
        head_norm(qg_ref[...] * SB_SCALE)

    @pl.when(jnp.logical_and(j >= sb_tiles, j < 2 * sb_tiles))
    def _():
        head_norm(kg_ref[...])

    @pl.when(jnp.logical_and(j >= 2 * sb_tiles, j < 3 * sb_tiles))
    def _():
        qkv_ref[...] = acc.astype(qkv_ref.dtype)

    @pl.when(jnp.logical_and(j >= 3 * sb_tiles, j < last_tile))
    def _():
        p_ref[...] = acc

    @pl.when(j == last_tile)
    def _():
        lane = lax.broadcasted_iota(jnp.int32, acc.shape, 1)
        p_ref[...] = jnp.where(lane < last_valid, acc, 0.0)


def _in_proj(xg, ss, w_in, layer, lay, q_gain, k_gain):
    n, d = xg.shape
    src, dst = lay["src"], lay["dst"]
    tn = INPROJ_TN
    nt = -(-lay["in_w"] // tn)
    sb_tiles = lay["sb_w"] // tn
    assert lay["sb_w"] % tn == 0 and src["sb_q"][0] == 0 and src["sb_k"][0] == lay["sb_w"]
    col_dst = np.full(nt * tn, -1)
    for nm in dst:
        o, sz = src[nm]
        col_dst[o:o + sz] = dst[nm] + np.arange(sz)
    p_tile = []
    for t in range(nt):
        cols = col_dst[t * tn:(t + 1) * tn]
        real = np.flatnonzero(cols >= 0)
        first = int(cols[real[0]] - real[0]) if real.size else 0
        assert first % tn == 0 and (cols[real] == first + real).all() and real.size in (0, tn, lay["in_w"] % tn)
        p_tile.append(first // tn)
    q_tile = [min(t, 3 * sb_tiles - 1) for t in range(nt)]
    tm = _tile(n, 2048)
    kern = functools.partial(_in_proj_kernel, sb_tiles=sb_tiles, last_tile=nt - 1,
                             last_valid=lay["in_w"] - (nt - 1) * tn)
    grid_spec = pltpu.PrefetchScalarGridSpec(
        num_scalar_prefetch=2,
        grid=(n // tm, nt),
        in_specs=[pl.BlockSpec((tm, d), lambda i, j, pt, qt: (i, 0), pipeline_mode=pl.Buffered(1)),
                  pl.BlockSpec((tm, LANES), lambda i, j, pt, qt: (i, 0)),
                  pl.BlockSpec((None, d, tn), lambda i, j, pt, qt: (layer, 0, j)),
                  pl.BlockSpec((1, HEAD_DIM), lambda i, j, pt, qt: (0, 0)),
                  pl.BlockSpec((1, HEAD_DIM), lambda i, j, pt, qt: (0, 0))],
        out_specs=[pl.BlockSpec((tm, tn), lambda i, j, pt, qt: (i, pt[j])),
                   pl.BlockSpec((tm, tn), lambda i, j, pt, qt: (i, qt[j]))])
    return pl.pallas_call(
        kern,
        grid_spec=grid_spec,
        out_shape=[jax.ShapeDtypeStruct((n, lay["p_w"]), F32),
                   jax.ShapeDtypeStruct((n, 3 * lay["sb_w"]), BF16)],
        compiler_params=_params("parallel", "arbitrary"),
        name="mix_in_proj",
    )(jnp.asarray(p_tile, jnp.int32), jnp.asarray(q_tile, jnp.int32), xg, ss, w_in,
      q_gain.reshape(1, HEAD_DIM), k_gain.reshape(1, HEAD_DIM))


def _out_proj_kernel(ya_ref, yb_ref, yc_ref, w_ref, x_ref, g_ref, o_ref, xg_ref, ss_ref, y_ref):
    ka, kb = ya_ref.shape[1], yb_ref.shape[1]

    @pl.when(pl.program_id(1) == 0)
    def _():
        y_ref[:, 0:ka] = ya_ref[...]
        y_ref[:, ka:ka + kb] = yb_ref[...]
        y_ref[:, ka + kb:] = yc_ref[...]

    y = x_ref[...] + jnp.dot(y_ref[...], w_ref[...], preferred_element_type=F32)
    _store_residual(y, g_ref, o_ref, xg_ref, ss_ref)


def _out_proj(ya, yb, yc, w, layer, x2, g_next):
    n, d = x2.shape
    tm = _tile(n, 1024)
    tn = _tile(d, 512)
    ka, kb, kc = ya.shape[1], yb.shape[1], yc.shape[1]
    kw = w.shape[1]
    assert ka + kb + kc == kw
    out_specs, out_shape = _residual_out(n, d, tm, tn, True)
    return pl.pallas_call(
        _out_proj_kernel,
        grid=(n // tm, d // tn),
        in_specs=[pl.BlockSpec((tm, ka), lambda i, j: (i, 0)),
                  pl.BlockSpec((tm, kb), lambda i, j: (i, 0)),
                  pl.BlockSpec((tm, kc), lambda i, j: (i, 0)),
                  pl.BlockSpec((None, kw, tn), lambda i, j: (layer, 0, j)),
                  pl.BlockSpec((tm, tn), lambda i, j: (i, j)),
                  pl.BlockSpec((1, tn), lambda i, j: (0, j))],
        out_specs=out_specs, out_shape=out_shape,
        scratch_shapes=[pltpu.VMEM((tm, kw), BF16)],
        compiler_params=_params("parallel", "arbitrary"),
        name="mix_out_proj",
    )(ya, yb, yc, w, x2, g_next.reshape(1, d))


SB_T = 256
SB_HEADS_PER_STEP = 4
SB_DEAD = 105.0


def _sb_attn_kernel(q_ref, k_ref, v_ref, o_ref):
    t = SB_T
    i = pl.program_id(2)
    heads = range(q_ref.shape[1] // HEAD_DIM)
    row = lax.broadcasted_iota(jnp.int32, (t, t), 0)
    col = lax.broadcasted_iota(jnp.int32, (t, t), 1)
    tri = jnp.where(row >= col, 1.0, 0.0).astype(BF16)

    def block(h, kb, carry, limit):
        acc, run = carry
        cols = slice(h * HEAD_DIM, (h + 1) * HEAD_DIM)
        start = pl.multiple_of(kb * t, t)
        k = k_ref[pl.ds(start, t), cols]
        v = v_ref[pl.ds(start, t), cols]
        z = lax.dot_general(q_ref[:, cols], k, (((1,), (1,)), ((), ())), preferred_element_type=F32)
        sp = _softplus(z)
        if limit is not None:
            keep = col < limit
            sp = jnp.where(keep, sp, 0.0)
        hi, lo = _split_bf16(sp, 2)
        c = (jnp.dot(hi, tri, preferred_element_type=F32)
             + jnp.dot(lo, tri, preferred_element_type=F32))
        e = z - c - run
        if limit is not None:
            e = jnp.where(keep, e, NEG_BIG)
        a = jnp.exp(e)
        acc = acc + jnp.dot(a.astype(BF16), v, preferred_element_type=F32)
        return acc, run + c[:, :1]

    carries = [(jnp.zeros((t, HEAD_DIM), F32), jnp.zeros((t, 1), F32)) for _ in heads]
    carries = [block(h, i, carries[h], row) for h in heads]
    carries = [block(h, jnp.maximum(i - 1, 0), carries[h], jnp.where(i >= 1, t, 0)) for h in heads]

    def run_min(carries):
        return functools.reduce(jnp.minimum, [jnp.min(run) for _, run in carries])

    def live(st):
        kb, _, low = st
        return jnp.logical_and(kb >= 0, low < SB_DEAD)

    def step(st):
        kb, carries, _ = st
        carries = [block(h, kb, carries[h], None) for h in heads]
        return kb - 1, carries, run_min(carries)

    _, carries, _ = lax.while_loop(live, step, (i - 2, carries, run_min(carries)))
    for h in heads:
        o_ref[:, h * HEAD_DIM:(h + 1) * HEAD_DIM] = carries[h][0].astype(o_ref.dtype)


def _sb_attention(qkv, batch, seq, heads):
    w = heads * HEAD_DIM
    q3 = qkv.reshape(batch, seq, 3 * w)
    k3 = v3 = q3
    hps = SB_HEADS_PER_STEP if heads % SB_HEADS_PER_STEP == 0 else 1
    groups, gw = heads // hps, hps * HEAD_DIM
    assert seq % SB_T == 0
    out = pl.pallas_call(
        _sb_attn_kernel,
        grid=(batch, groups, seq // SB_T),
        in_specs=[pl.BlockSpec((None, SB_T, gw), lambda b, g, i: (b, i, g)),
                  pl.BlockSpec((None, seq, gw), lambda b, g, i: (b, 0, groups + g)),
                  pl.BlockSpec((None, seq, gw), lambda b, g, i: (b, 0, 2 * groups + g))],
        out_specs=pl.BlockSpec((None, SB_T, gw), lambda b, g, i: (b, i, g)),
        out_shape=jax.ShapeDtypeStruct((batch, seq, w), BF16),
        compiler_params=_params("parallel", "parallel", "arbitrary"),
        name="sb_attention",
    )(q3, k3, v3)
    return out.reshape(batch * seq, w)


CONV_TS = 512
CONV_HALO = 32
CONV_ROWS = 32


def _conv_kernel(val_ref, gate_ref, hval_ref, hgate_ref, w_ref, b_ref, lg_ref, lb_ref, o_ref,
                 u_ref, us_ref):
    ts = o_ref.shape[0]
    i = pl.program_id(1)
    halo = hval_ref[...] * _sigmoid(hgate_ref[...])
    u_ref[0:CONV_HALO, :] = jnp.where(i > 0, halo, 0.0)
    u_ref[CONV_HALO:, :] = val_ref[...] * _sigmoid(gate_ref[...])
    span = CONV_HALO + ts - SUBLANES
    for m in range(1, SUBLANES):
        us_ref[m - 1, 0:span, :] = u_ref[m:m + span, :]
    base = CONV_HALO - (CONV_K - 1)
    bias, lg, lb = b_ref[...], lg_ref[...], lb_ref[...]
    width = o_ref.shape[1]
    for r0 in range(0, ts, CONV_ROWS):
        acc = jnp.zeros((CONV_ROWS // SUBLANES, SUBLANES, width), F32)
        for k in range(CONV_K):
            vreg_row, m = divmod(base + k, SUBLANES)
            lo = vreg_row * SUBLANES + r0
            src = u_ref[lo:lo + CONV_ROWS, :] if m == 0 else us_ref[m - 1, lo:lo + CONV_ROWS, :]
            acc = acc + w_ref[k] * src.reshape(acc.shape)
        acc = acc.reshape(CONV_ROWS, width) + bias
        mu = jnp.mean(acc, axis=-1, keepdims=True)
        cen = acc - mu
        var = jnp.mean(cen * cen, axis=-1, keepdims=True)
        y = cen * lax.rsqrt(var + EPS) * lg + lb
        o_ref[r0:r0 + CONV_ROWS, :] = (y * _sigmoid(y)).astype(o_ref.dtype)


def _conv_module(p, batch, seq, val_off, gate_off, width, conv_w, conv_b, ln_g, ln_b):
    p3 = p.reshape(batch, seq, p.shape[1])
    ts = _tile(seq, CONV_TS)
    assert val_off % width == 0 and gate_off % width == 0 and ts % CONV_HALO == 0
    cv, cg = val_off // width, gate_off // width
    hpt = ts // CONV_HALO
    main = lambda c: pl.BlockSpec((None, ts, width), lambda b, i: (b, i, c))
    halo = lambda c: pl.BlockSpec((None, CONV_HALO, width),
                                  lambda b, i: (b, jnp.maximum(i * hpt - 1, 0), c))
    vec = lambda r: pl.BlockSpec((r, width), lambda b, i: (0, 0))
    out = pl.pallas_call(
        _conv_kernel,
        grid=(batch, seq // ts),
        in_specs=[main(cv), main(cg), halo(cv), halo(cg),
                  pl.BlockSpec((CONV_K, SUBLANES, width), lambda b, i: (0, 0, 0)), vec(1), vec(1), vec(1)],
        out_specs=pl.BlockSpec((None, ts, width), lambda b, i: (b, i, 0)),
        out_shape=jax.ShapeDtypeStruct((batch, seq, width), BF16),
        scratch_shapes=[pltpu.VMEM((CONV_HALO + ts, width), F32),
                        pltpu.VMEM((SUBLANES - 1, CONV_HALO + ts, width), F32)],
        compiler_params=_params("parallel", "arbitrary"),
        name="conv_module",
    )(p3, p3, p3, p3, jnp.broadcast_to(conv_w[:, None, :], (CONV_K, SUBLANES, width)),
      conv_b.reshape(1, width), ln_g.reshape(1, width), ln_b.reshape(1, width))
    return out.reshape(batch * seq, width)


GLA_ROWS = 512
GLA_SUB = 16


def _gla_kernel(q_ref, k_ref, v_ref, r_ref, lr_ref, gw_ref, gb_ref, og_ref, o_ref, state_ref):
    c, sb = GLA_CHUNK, GLA_SUB
    rows = q_ref.shape[0]
    nb, spc = rows // sb, c // sb

    @pl.when(pl.program_id(2) == 0)
    def _():
        state_ref[...] = jnp.zeros_like(state_ref)

    lr_parts = _split_bf16(lr_ref[...], 2)
    gw_parts = _split_bf16(gw_ref[...], 2)
    logit = (jnp.dot(lr_parts[0], gw_parts[0], preferred_element_type=F32)
             + jnp.dot(lr_parts[1], gw_parts[0], preferred_element_type=F32)
             + jnp.dot(lr_parts[0], gw_parts[1], preferred_element_type=F32)) + gb_ref[...]
    g = -_softplus(-logit) * (1.0 / GLA_TAU)

    row = lax.broadcasted_iota(jnp.int32, (c, c), 0)
    col = lax.broadcasted_iota(jnp.int32, (c, c), 1)
    tril = jnp.where(row >= col, 1.0, 0.0).astype(BF16)
    g_parts = _split_bf16(g, 3)
    b = jnp.concatenate(
        [sum(jnp.dot(tril, part[r0:r0 + c, :], preferred_element_type=F32) for part in g_parts)
         for r0 in range(0, rows, c)], axis=0)

    q = q_ref[...] * GLA_SCALE
    k = k_ref[...]
    b3, q3, k3 = (a.reshape(nb, sb, GLA_DK) for a in (b, q, k))
    r3 = (b3 - g.reshape(nb, sb, GLA_DK))[:, 0:1, :]
    q_sub = (q3 * jnp.exp(b3 - r3)).reshape(rows, GLA_DK).astype(BF16)

    ones = jnp.ones((GLA_DK, LANES), BF16)
    half = sb // 2
    t_half = lax.broadcasted_iota(jnp.int32, (nb, half, GLA_DK), 1)
    lane = lax.broadcasted_iota(jnp.int32, (nb, half, LANES), 2)
    sub = lax.broadcasted_iota(jnp.int32, (nb, half, LANES), 0) % (c // sb)
    off_lo = off_hi = lane - sub * sb
    halves = lambda a: (a[:, :half, :], a[:, half:, :])
    (b_lo, b_hi), (q_lo, q_hi) = halves(b3), halves(q3)

    def row_sums(qh, bh, s, first_row):
        e = bh - b3[:, s:s + 1, :]
        if s > first_row:
            e = jnp.where(t_half >= s - first_row, e, NEG_BIG)
        prod = (qh * k3[:, s:s + 1, :] * jnp.exp(e)).reshape(nb * half, GLA_DK).astype(BF16)
        return jnp.dot(prod, ones, preferred_element_type=F32).reshape(nb, half, LANES)

    diag_lo = jnp.zeros((nb, half, LANES), F32)
    diag_hi = jnp.zeros((nb, half, LANES), F32)
    for s in range(sb):
        if s < half:
            diag_lo = jnp.where(off_lo == s, row_sums(q_lo, b_lo, s, 0), diag_lo)
        diag_hi = jnp.where(off_hi == s, row_sums(q_hi, b_hi, s, half), diag_hi)
    diag = jnp.concatenate([diag_lo, diag_hi], axis=1).reshape(rows, LANES)

    kidx = lax.broadcasted_iota(jnp.int32, (c, GLA_DK), 0)
    og = og_ref[...]
    for ci in range(rows // c):
        lo = ci * c
        bc, kc = b[lo:lo + c, :], k[lo:lo + c, :]
        vc = v_ref[lo:lo + c, :].astype(BF16)
        parts = [jnp.zeros((sb, c), F32)]
        for i in range(1, spc):
            r_i = r3[ci * spc + i]
            k_i = (kc * jnp.exp(jnp.where(kidx < i * sb, r_i - bc, NEG_BIG))).astype(BF16)
            parts.append(lax.dot_general(q_sub[lo + i * sb:lo + (i + 1) * sb, :], k_i,
                                         (((1,), (1,)), ((), ())), preferred_element_type=F32))
        scores = jnp.concatenate(parts, axis=0) + diag[lo:lo + c, :c]

        state = state_ref[...]
        o = jnp.dot(scores.astype(BF16), vc, preferred_element_type=F32)
        o = o + lax.dot_general((q[lo:lo + c, :] * jnp.exp(bc)).astype(BF16), state.astype(BF16),
                                (((1,), (1,)), ((), ())), preferred_element_type=F32)
        b_last = bc[c - 1:c, :]
        k_dec = (kc * jnp.exp(b_last - bc)).astype(BF16)
        state_ref[...] = jnp.exp(b_last) * state + lax.dot_general(
            vc, k_dec, (((0,), (0,)), ((), ())), preferred_element_type=F32)

        ms = jnp.mean(o * o, axis=-1, keepdims=True)
        o = o * lax.rsqrt(ms + EPS) * og
        gate = r_ref[lo:lo + c, :]
        o_ref[lo:lo + c, :] = (o * gate * _sigmoid(gate)).astype(o_ref.dtype)


def _gla(p, batch, seq, heads, q_off, k_off, v_off, r_off, lr_off, gate_w_p, gate_b, out_gain):
    p3 = p.reshape(batch, seq, p.shape[1])
    rows = _tile(seq, GLA_ROWS)
    assert rows % GLA_CHUNK == 0
    assert q_off % GLA_DK == 0 and k_off % GLA_DK == 0 and lr_off % LANES == 0
    assert v_off % GLA_DV == 0 and r_off % GLA_DV == 0
    spec = lambda width, blk0: pl.BlockSpec((None, rows, width), lambda b, h, i: (b, i, blk0 + h))
    out = pl.pallas_call(
        _gla_kernel,
        grid=(batch, heads, seq // rows),
        in_specs=[spec(GLA_DK, q_off // GLA_DK), spec(GLA_DK, k_off // GLA_DK),
                  spec(GLA_DV, v_off // GLA_DV), spec(GLA_DV, r_off // GLA_DV),
                  pl.BlockSpec((None, rows, LANES), lambda b, h, i: (b, i, lr_off // LANES)),
                  pl.BlockSpec((LANES, GLA_DK), lambda b, h, i: (0, h)),
                  pl.BlockSpec((1, GLA_DK), lambda b, h, i: (0, h)),
                  pl.BlockSpec((1, GLA_DV), lambda b, h, i: (0, 0))],
        out_specs=pl.BlockSpec((None, rows, GLA_DV), lambda b, h, i: (b, i, h)),
        out_shape=jax.ShapeDtypeStruct((batch, seq, heads * GLA_DV), BF16),
        scratch_shapes=[pltpu.VMEM((GLA_DV, GLA_DK), F32)],
        compiler_params=_params("parallel", "parallel", "arbitrary"),
        name="gla",
    )(p3, p3, p3, p3, p3, gate_w_p, gate_b.reshape(1, -1), out_gain.reshape(1, GLA_DV))
    return out.reshape(batch * seq, heads * GLA_DV)


def _layout(d_model):
    sb_w = 3 * d_model // 8
    conv_w = d_model // 4
    gla_w = d_model - sb_w - conv_w
    gla_heads = gla_w // GLA_DV
    key_w = gla_heads * GLA_DK
    names = ("sb_q", "sb_k", "sb_v", "conv_val", "conv_gate", "gla_q", "gla_k", "gla_v", "gla_r", "gla_lr")
    sizes = (sb_w, sb_w, sb_w, conv_w, conv_w, key_w, key_w, gla_w, gla_w, GLA_RANK)
    src, o = {}, 0
    for nm, sz in zip(names, sizes):
        src[nm] = (o, sz)
        o += sz
    order = ("conv_val", "conv_gate", "gla_v", "gla_r", "gla_q", "gla_k", "gla_lr")
    dst, o = {}, 0
    for nm in order:
        dst[nm] = o
        o += src[nm][1]
    o = dst["gla_lr"] + INPROJ_TN
    return dict(sb_w=sb_w, conv_w=conv_w, gla_w=gla_w, sb_heads=sb_w // HEAD_DIM,
                gla_heads=gla_heads, src=src, dst=dst, p_w=o, in_w=sum(sizes))


def kernel(x, ffn1_norm, ffn1_w_in, ffn1_w_out, mix_norm, mix_w_in, sb_q_gain, sb_k_gain, conv_w, conv_b, conv_ln_g, conv_ln_b, gla_gate_w, gla_gate_b, gla_out_gain, mix_w_out, ffn2_norm, ffn2_w_in, ffn2_w_out, final_norm):
    batch, seq, d = x.shape
    depth = ffn1_norm.shape[0]
    lay = _layout(d)
    dst = lay["dst"]
    assert mix_w_in.shape[2] == lay["in_w"]

    ffn1_w_out, ffn2_w_out, mix_w_out, mix_w_in = (
        w.astype(BF16) for w in (ffn1_w_out, ffn2_w_out, mix_w_out, mix_w_in))
    x2 = x.reshape(batch * seq, d)
    xg, ss = _norm(x2, None, ffn1_norm[0])
    for l in range(depth):
        x2, xg, ss = _ffn(x2, xg, ss, ffn1_w_in, ffn1_w_out, l, mix_norm[l])

        p, qkv = _in_proj(xg, ss, mix_w_in, l, lay, sb_q_gain[l], sb_k_gain[l])
        ya = _sb_attention(qkv, batch, seq, lay["sb_heads"])
        yb = _conv_module(p, batch, seq, dst["conv_val"], dst["conv_gate"], lay["conv_w"],
                          conv_w[l], conv_b[l], conv_ln_g[l], conv_ln_b[l])
        gate_w_p = jnp.pad(gla_gate_w[l], ((0, LANES - GLA_RANK), (0, 0)))
        yc = _gla(p, batch, seq, lay["gla_heads"], dst["gla_q"], dst["gla_k"], dst["gla_v"],
                  dst["gla_r"], dst["gla_lr"], gate_w_p, gla_gate_b[l], gla_out_gain[l])
        x2, xg, ss = _out_proj(ya, yb, yc, mix_w_out, l, x2, ffn2_norm[l])

        (x2,) = _ffn(x2, xg, ss, ffn2_w_in, ffn2_w_out, l, None)
        if l + 1 < depth:
            x2, xg, ss = _norm(x2, final_norm[l], ffn1_norm[l + 1])
        else:
            (x2,) = _norm(x2, final_norm[l], None)
    return x2.reshape(batch, seq, d)
```

```python
import functools

import jax
import jax.numpy as jnp
import numpy as np
from jax import lax
from jax.experimental import pallas as pl
from jax.experimental.pallas import tpu as pltpu

F32 = jnp.float32
BF16 = jnp.bfloat16

HEAD_DIM = 128
GLA_DK = 128
GLA_DV = 256
GLA_RANK = 16
GLA_TAU = 16.0
GLA_CHUNK = 64
CONV_K = 31
EPS = 1e-6
SB_SCALE = HEAD_DIM ** -0.5
GLA_SCALE = GLA_DK ** -0.5

LANES = 128
SUBLANES = 8
VMEM_LIMIT_BYTES = 56 * 1024 * 1024

NEG_BIG = -1e30


def _params(*sem):
    return pltpu.CompilerParams(dimension_semantics=sem, vmem_limit_bytes=VMEM_LIMIT_BYTES)


def _tile(n, want):
    if n <= want:
        return n
    t = want
    while n % t:
        t -= SUBLANES
    return t


def _split_bf16(x, terms):
    out = []
    r = x
    for _ in range(terms):
        h = r.astype(BF16)
        out.append(h)
        r = r - h.astype(F32)
    return out


def _softplus(z):
    return jnp.maximum(z, 0.0) + jnp.log(1.0 + jnp.exp(-jnp.abs(z)))


def _sigmoid(z):
    return 1.0 / (1.0 + jnp.exp(-z))


def _row_scale(ss_ref, d):
    return lax.rsqrt(ss_ref[:, 0:1] * (1.0 / d) + EPS)


def _store_residual(y, g_ref, o_ref, xg_ref, ss_ref):
    o_ref[...] = y
    if xg_ref is None:
        return
    xg_ref[...] = (y * g_ref[...]).astype(xg_ref.dtype)
    part = jnp.broadcast_to(jnp.sum(y * y, axis=-1, keepdims=True), ss_ref.shape)
    j = pl.program_id(1)

    @pl.when(j == 0)
    def _():
        ss_ref[...] = part

    @pl.when(j > 0)
    def _():
        ss_ref[...] += part


def _residual_out(n, d, tm, tn, with_norm):
    specs = [pl.BlockSpec((tm, tn), lambda i, j: (i, j))]
    shapes = [jax.ShapeDtypeStruct((n, d), F32)]
    if with_norm:
        specs += [pl.BlockSpec((tm, tn), lambda i, j: (i, j)), pl.BlockSpec((tm, LANES), lambda i, j: (i, 0))]
        shapes += [jax.ShapeDtypeStruct((n, d), BF16), jax.ShapeDtypeStruct((n, LANES), F32)]
    return specs, shapes


def _norm_kernel(*refs, d, with_out, with_pair):
    refs = list(refs)
    x_ref = refs.pop(0)
    x = x_ref[...]
    if with_out:
        g_ref = refs.pop(0)
    if with_pair:
        gn_ref = refs.pop(0)
    if with_out:
        o_ref = refs.pop(0)
        ms = jnp.mean(x * x, axis=-1, keepdims=True)
        x = x * lax.rsqrt(ms + EPS) * g_ref[...]
        o_ref[...] = x
    if with_pair:
        xg_ref, ss_ref = refs
        xg_ref[...] = (x * gn_ref[...]).astype(xg_ref.dtype)
        ss_ref[...] = jnp.broadcast_to(jnp.sum(x * x, axis=-1, keepdims=True), ss_ref.shape)


def _norm(x2, g_out, g_pair):
    n, d = x2.shape
    tm = _tile(n, 256)
    row = pl.BlockSpec((tm, d), lambda i: (i, 0))
    vec = pl.BlockSpec((1, d), lambda i: (0, 0))
    ins, in_specs, out_specs, out_shape = [x2], [row], [], []
    if g_out is not None:
        ins.append(g_out.reshape(1, d)); in_specs.append(vec)
        out_specs.append(row); out_shape.append(jax.ShapeDtypeStruct((n, d), F32))
    if g_pair is not None:
        ins.append(g_pair.reshape(1, d)); in_specs.append(vec)
        out_specs += [row, pl.BlockSpec((tm, LANES), lambda i: (i, 0))]
        out_shape += [jax.ShapeDtypeStruct((n, d), BF16), jax.ShapeDtypeStruct((n, LANES), F32)]
    return pl.pallas_call(
        functools.partial(_norm_kernel, d=d, with_out=g_out is not None, with_pair=g_pair is not None),
        grid=(n // tm,),
        in_specs=in_specs, out_specs=out_specs, out_shape=out_shape,
        compiler_params=_params("parallel"),
        name="rmsnorm",
    )(*ins)


FFN_TF = 256


def _ffn_up_kernel(xg_ref, ss_ref, wg_ref, wu_ref, o_ref):
    xg = xg_ref[...]
    s = _row_scale(ss_ref, xg.shape[1])
    g = s * jnp.dot(xg, wg_ref[...].astype(BF16), preferred_element_type=F32)
    u = s * jnp.dot(xg, wu_ref[...].astype(BF16), preferred_element_type=F32)
    o_ref[...] = (g * _sigmoid(g) * u).astype(o_ref.dtype)


def _ffn_down_kernel(a_ref, w_ref, x_ref, *rest):
    g_ref, o_ref, xg_ref, ss_ref = rest if len(rest) == 4 else (None, rest[0], None, None)
    acc = jnp.dot(a_ref[...], w_ref[...], preferred_element_type=F32)
    _store_residual(x_ref[...] + 0.5 * acc, g_ref, o_ref, xg_ref, ss_ref)


def _ffn(x2, xg, ss, w_in, w_out, layer, g_next):
    n, d = x2.shape
    f = w_out.shape[1]
    assert w_in.shape[1:] == (d, 2 * f) and f % FFN_TF == 0
    nj = f // FFN_TF
    tm = _tile(n, 2048)
    a = pl.pallas_call(
        _ffn_up_kernel,
        grid=(n // tm, nj),
        in_specs=[pl.BlockSpec((tm, d), lambda i, j: (i, 0), pipeline_mode=pl.Buffered(1)),
                  pl.BlockSpec((tm, LANES), lambda i, j: (i, 0)),
                  pl.BlockSpec((None, d, FFN_TF), lambda i, j: (layer, 0, j)),
                  pl.BlockSpec((None, d, FFN_TF), lambda i, j: (layer, 0, nj + j))],
        out_specs=pl.BlockSpec((tm, FFN_TF), lambda i, j: (i, j)),
        out_shape=jax.ShapeDtypeStruct((n, f), BF16),
        compiler_params=_params("parallel", "arbitrary"),
        name="ffn_up",
    )(xg, ss, w_in, w_in)
    tm2 = _tile(n, 512)
    tn = _tile(d, 512)
    with_norm = g_next is not None
    out_specs, out_shape = _residual_out(n, d, tm2, tn, with_norm)
    ins = [a, w_out, x2] + ([g_next.reshape(1, d)] if with_norm else [])
    in_specs = [pl.BlockSpec((tm2, f), lambda i, j: (i, 0)),
                pl.BlockSpec((None, f, tn), lambda i, j: (layer, 0, j)),
                pl.BlockSpec((tm2, tn), lambda i, j: (i, j))]
    if with_norm:
        in_specs.append(pl.BlockSpec((1, tn), lambda i, j: (0, j)))
    return pl.pallas_call(
        _ffn_down_kernel,
        grid=(n // tm2, d // tn),
        in_specs=in_specs, out_specs=out_specs, out_shape=out_shape,
        compiler_params=_params("parallel", "arbitrary"),
        name="ffn_down",
    )(*ins)


INPROJ_TN = 512


def _in_proj_kernel(pt_ref, qt_ref, xg_ref, ss_ref, w_ref, qg_ref, kg_ref, p_ref, qkv_ref, *,
                    sb_tiles, last_tile, last_valid):
    del pt_ref, qt_ref
    j = pl.program_id(1)
    xg = xg_ref[...]
    acc = _row_scale(ss_ref, xg.shape[1]) * lax.dot_general(
        xg, w_ref[...].astype(BF16), (((1,), (1,)), ((), ())), preferred_element_type=F32)

    def head_norm(gain):
        for c0 in range(0, INPROJ_TN, HEAD_DIM):
            x = acc[:, c0:c0 + HEAD_DIM]
            ms = jnp.mean(x * x, axis=-1, keepdims=True)
            qkv_ref[:, c0:c0 + HEAD_DIM] = (x * lax.rsqrt(ms + EPS) * gain).astype(qkv_ref.dtype)

    @pl.when(j < sb_tiles)
    def _():
        head_norm(qg_ref[...] * SB_SCALE)

    @pl.when(jnp.logical_and(j >= sb_tiles, j < 2 * sb_tiles))
    def _():
        head_norm(kg_ref[...])

    @pl.when(jnp.logical_and(j >= 2 * sb_tiles, j < 3 * sb_tiles))
    def _():
        qkv_ref[...] = acc.astype(qkv_ref.dtype)

    @pl.when(jnp.logical_and(j >= 3 * sb_tiles, j < last_tile))
    def _():
        p_ref[...] = acc

    @pl.when(j == last_tile)
    def _():
        lane = lax.broadcasted_iota(jnp.int32, acc.shape, 1)
        p_ref[...] = jnp.where(lane < last_valid, acc, 0.0)


def _in_proj(xg, ss, w_in, layer, lay, q_gain, k_gain):
    n, d = xg.shape
    src, dst = lay["src"], lay["dst"]
    tn = INPROJ_TN
    nt = -(-lay["in_w"] // tn)
    sb_tiles = lay["sb_w"] // tn
    assert lay["sb_w"] % tn == 0 and src["sb_q"][0] == 0 and src["sb_k"][0] == lay["sb_w"]
    col_dst = np.full(nt * tn, -1)
    for nm in dst:
        o, sz = src[nm]
        col_dst[o:o + sz] = dst[nm] + np.arange(sz)
    p_tile = []
    for t in range(nt):
        cols = col_dst[t * tn:(t + 1) * tn]
        real = np.flatnonzero(cols >= 0)
        first = int(cols[real[0]] - real[0]) if real.size else 0
        assert first % tn == 0 and (cols[real] == first + real).all() and real.size in (0, tn, lay["in_w"] % tn)
        p_tile.append(first // tn)
    q_tile = [min(t, 3 * sb_tiles - 1) for t in range(nt)]
    tm = _tile(n, 2048)
    kern = functools.partial(_in_proj_kernel, sb_tiles=sb_tiles, last_tile=nt - 1,
                             last_valid=lay["in_w"] - (nt - 1) * tn)
    grid_spec = pltpu.PrefetchScalarGridSpec(
        num_scalar_prefetch=2,
        grid=(n // tm, nt),
        in_specs=[pl.BlockSpec((tm, d), lambda i, j, pt, qt: (i, 0), pipeline_mode=pl.Buffered(1)),
                  pl.BlockSpec((tm, LANES), lambda i, j, pt, qt: (i, 0)),
                  pl.BlockSpec((None, tn, d), lambda i, j, pt, qt: (layer, j, 0)),
                  pl.BlockSpec((1, HEAD_DIM), lambda i, j, pt, qt: (0, 0)),
                  pl.BlockSpec((1, HEAD_DIM), lambda i, j, pt, qt: (0, 0))],
        out_specs=[pl.BlockSpec((tm, tn), lambda i, j, pt, qt: (i, pt[j])),
                   pl.BlockSpec((tm, tn), lambda i, j, pt, qt: (i, qt[j]))])
    return pl.pallas_call(
        kern,
        grid_spec=grid_spec,
        out_shape=[jax.ShapeDtypeStruct((n, lay["p_w"]), F32),
                   jax.ShapeDtypeStruct((n, 3 * lay["sb_w"]), BF16)],
        compiler_params=_params("parallel", "arbitrary"),
        name="mix_in_proj",
    )(jnp.asarray(p_tile, jnp.int32), jnp.asarray(q_tile, jnp.int32), xg, ss, w_in,
      q_gain.reshape(1, HEAD_DIM), k_gain.reshape(1, HEAD_DIM))


def _out_proj_kernel(ya_ref, yb_ref, yc_ref, w_ref, x_ref, g_ref, o_ref, xg_ref, ss_ref, y_ref):
    ka, kb = ya_ref.shape[1], yb_ref.shape[1]

    @pl.when(pl.program_id(1) == 0)
    def _():
        y_ref[:, 0:ka] = ya_ref[...]
        y_ref[:, ka:ka + kb] = yb_ref[...]
        y_ref[:, ka + kb:] = yc_ref[...]

    y = x_ref[...] + jnp.dot(y_ref[...], w_ref[...], preferred_element_type=F32)
    _store_residual(y, g_ref, o_ref, xg_ref, ss_ref)


def _out_proj(ya, yb, yc, w, layer, x2, g_next):
    n, d = x2.shape
    tm = _tile(n, 1024)
    tn = _tile(d, 512)
    ka, kb, kc = ya.shape[1], yb.shape[1], yc.shape[1]
    kw = w.shape[1]
    assert ka + kb + kc == kw
    out_specs, out_shape = _residual_out(n, d, tm, tn, True)
    return pl.pallas_call(
        _out_proj_kernel,
        grid=(n // tm, d // tn),
        in_specs=[pl.BlockSpec((tm, ka), lambda i, j: (i, 0)),
                  pl.BlockSpec((tm, kb), lambda i, j: (i, 0)),
                  pl.BlockSpec((tm, kc), lambda i, j: (i, 0)),
                  pl.BlockSpec((None, kw, tn), lambda i, j: (layer, 0, j)),
                  pl.BlockSpec((tm, tn), lambda i, j: (i, j)),
                  pl.BlockSpec((1, tn), lambda i, j: (0, j))],
        out_specs=out_specs, out_shape=out_shape,
        scratch_shapes=[pltpu.VMEM((tm, kw), BF16)],
        compiler_params=_params("parallel", "arbitrary"),
        name="mix_out_proj",
    )(ya, yb, yc, w, x2, g_next.reshape(1, d))


SB_T = 256
SB_HEADS_PER_STEP = 4
SB_DEAD = 105.0


def _sb_attn_kernel(q_ref, k_ref, v_ref, o_ref):
    t = SB_T
    i = pl.program_id(2)
    heads = range(q_ref.shape[1] // HEAD_DIM)
    row = lax.broadcasted_iota(jnp.int32, (t, t), 0)
    col = lax.broadcasted_iota(jnp.int32, (t, t), 1)
    tri = jnp.where(row >= col, 1.0, 0.0).astype(BF16)

    def block(h, kb, carry, limit):
        acc, run = carry
        cols = slice(h * HEAD_DIM, (h + 1) * HEAD_DIM)
        start = pl.multiple_of(kb * t, t)
        k = k_ref[pl.ds(start, t), cols]
        v = v_ref[pl.ds(start, t), cols]
        z = lax.dot_general(q_ref[:, cols], k, (((1,), (1,)), ((), ())), preferred_element_type=F32)
        sp = _softplus(z)
        if limit is not None:
            keep = col < limit
            sp = jnp.where(keep, sp, 0.0)
        hi, lo = _split_bf16(sp, 2)
        c = (jnp.dot(hi, tri, preferred_element_type=F32)
             + jnp.dot(lo, tri, preferred_element_type=F32))
        e = z - c - run
        if limit is not None:
            e = jnp.where(keep, e, NEG_BIG)
        a = jnp.exp(e)
        acc = acc + jnp.dot(a.astype(BF16), v, preferred_element_type=F32)
        return acc, run + c[:, :1]

    carries = [(jnp.zeros((t, HEAD_DIM), F32), jnp.zeros((t, 1), F32)) for _ in heads]
    carries = [block(h, i, carries[h], row) for h in heads]
    carries = [block(h, jnp.maximum(i - 1, 0), carries[h], jnp.where(i >= 1, t, 0)) for h in heads]

    def run_min(carries):
        return functools.reduce(jnp.minimum, [jnp.min(run) for _, run in carries])

    def live(st):
        kb, _, low = st
        return jnp.logical_and(kb >= 0, low < SB_DEAD)

    def step(st):
        kb, carries, _ = st
        carries = [block(h, kb, carries[h], None) for h in heads]
        return kb - 1, carries, run_min(carries)

    _, carries, _ = lax.while_loop(live, step, (i - 2, carries, run_min(carries)))
    for h in heads:
        o_ref[:, h * HEAD_DIM:(h + 1) * HEAD_DIM] = carries[h][0].astype(o_ref.dtype)


def _sb_attention(qkv, batch, seq, heads):
    w = heads * HEAD_DIM
    q3 = qkv.reshape(batch, seq, 3 * w)
    k3 = v3 = q3
    hps = SB_HEADS_PER_STEP if heads % SB_HEADS_PER_STEP == 0 else 1
    groups, gw = heads // hps, hps * HEAD_DIM
    assert seq % SB_T == 0
    out = pl.pallas_call(
        _sb_attn_kernel,
        grid=(batch, groups, seq // SB_T),
        in_specs=[pl.BlockSpec((None, SB_T, gw), lambda b, g, i: (b, i, g)),
                  pl.BlockSpec((None, seq, gw), lambda b, g, i: (b, 0, groups + g)),
                  pl.BlockSpec((None, seq, gw), lambda b, g, i: (b, 0, 2 * groups + g))],
        out_specs=pl.BlockSpec((None, SB_T, gw), lambda b, g, i: (b, i, g)),
        out_shape=jax.ShapeDtypeStruct((batch, seq, w), BF16),
        compiler_params=_params("parallel", "parallel", "arbitrary"),
        name="sb_attention",
    )(q3, k3, v3)
    return out.reshape(batch * seq, w)


CONV_TS = 512
CONV_HALO = 32
CONV_ROWS = 32


def _conv_kernel(val_ref, gate_ref, hval_ref, hgate_ref, w_ref, b_ref, lg_ref, lb_ref, o_ref,
                 u_ref, us_ref):
    ts = o_ref.shape[0]
    i = pl.program_id(1)
    halo = hval_ref[...] * _sigmoid(hgate_ref[...])
    u_ref[0:CONV_HALO, :] = jnp.where(i > 0, halo, 0.0)
    u_ref[CONV_HALO:, :] = val_ref[...] * _sigmoid(gate_ref[...])
    span = CONV_HALO + ts - SUBLANES
    for m in range(1, SUBLANES):
        us_ref[m - 1, 0:span, :] = u_ref[m:m + span, :]
    base = CONV_HALO - (CONV_K - 1)
    bias, lg, lb = b_ref[...], lg_ref[...], lb_ref[...]
    width = o_ref.shape[1]
    for r0 in range(0, ts, CONV_ROWS):
        acc = jnp.zeros((CONV_ROWS // SUBLANES, SUBLANES, width), F32)
        for k in range(CONV_K):
            vreg_row, m = divmod(base + k, SUBLANES)
            lo = vreg_row * SUBLANES + r0
            src = u_ref[lo:lo + CONV_ROWS, :] if m == 0 else us_ref[m - 1, lo:lo + CONV_ROWS, :]
            acc = acc + w_ref[k] * src.reshape(acc.shape)
        acc = acc.reshape(CONV_ROWS, width) + bias
        mu = jnp.mean(acc, axis=-1, keepdims=True)
        cen = acc - mu
        var = jnp.mean(cen * cen, axis=-1, keepdims=True)
        y = cen * lax.rsqrt(var + EPS) * lg + lb
        o_ref[r0:r0 + CONV_ROWS, :] = (y * _sigmoid(y)).astype(o_ref.dtype)


def _conv_module(p, batch, seq, val_off, gate_off, width, conv_w, conv_b, ln_g, ln_b):
    p3 = p.reshape(batch, seq, p.shape[1])
    ts = _tile(seq, CONV_TS)
    assert val_off % width == 0 and gate_off % width == 0 and ts % CONV_HALO == 0
    cv, cg = val_off // width, gate_off // width
    hpt = ts // CONV_HALO
    main = lambda c: pl.BlockSpec((None, ts, width), lambda b, i: (b, i, c))
    halo = lambda c: pl.BlockSpec((None, CONV_HALO, width),
                                  lambda b, i: (b, jnp.maximum(i * hpt - 1, 0), c))
    vec = lambda r: pl.BlockSpec((r, width), lambda b, i: (0, 0))
    out = pl.pallas_call(
        _conv_kernel,
        grid=(batch, seq // ts),
        in_specs=[main(cv), main(cg), halo(cv), halo(cg),
                  pl.BlockSpec((CONV_K, SUBLANES, width), lambda b, i: (0, 0, 0)), vec(1), vec(1), vec(1)],
        out_specs=pl.BlockSpec((None, ts, width), lambda b, i: (b, i, 0)),
        out_shape=jax.ShapeDtypeStruct((batch, seq, width), BF16),
        scratch_shapes=[pltpu.VMEM((CONV_HALO + ts, width), F32),
                        pltpu.VMEM((SUBLANES - 1, CONV_HALO + ts, width), F32)],
        compiler_params=_params("parallel", "arbitrary"),
        name="conv_module",
    )(p3, p3, p3, p3, jnp.broadcast_to(conv_w[:, None, :], (CONV_K, SUBLANES, width)),
      conv_b.reshape(1, width), ln_g.reshape(1, width), ln_b.reshape(1, width))
    return out.reshape(batch * seq, width)


GLA_ROWS = 512
GLA_SUB = 16


def _gla_kernel(q_ref, k_ref, v_ref, r_ref, lr_ref, gw_ref, gb_ref, og_ref, o_ref, state_ref):
    c, sb = GLA_CHUNK, GLA_SUB
    rows = q_ref.shape[0]
    nb, spc = rows // sb, c // sb

    @pl.when(pl.program_id(2) == 0)
    def _():
        state_ref[...] = jnp.zeros_like(state_ref)

    lr_parts = _split_bf16(lr_ref[...], 2)
    gw_parts = _split_bf16(gw_ref[...], 2)
    logit = (jnp.dot(lr_parts[0], gw_parts[0], preferred_element_type=F32)
             + jnp.dot(lr_parts[1], gw_parts[0], preferred_element_type=F32)
             + jnp.dot(lr_parts[0], gw_parts[1], preferred_element_type=F32)) + gb_ref[...]
    g = -_softplus(-logit) * (1.0 / GLA_TAU)

    row = lax.broadcasted_iota(jnp.int32, (c, c), 0)
    col = lax.broadcasted_iota(jnp.int32, (c, c), 1)
    tril = jnp.where(row >= col, 1.0, 0.0).astype(BF16)
    g_parts = _split_bf16(g, 3)
    b = jnp.concatenate(
        [sum(jnp.dot(tril, part[r0:r0 + c, :], preferred_element_type=F32) for part in g_parts)
         for r0 in range(0, rows, c)], axis=0)

    q = q_ref[...] * GLA_SCALE
    k = k_ref[...]
    b3, q3, k3 = (a.reshape(nb, sb, GLA_DK) for a in (b, q, k))
    r3 = (b3 - g.reshape(nb, sb, GLA_DK))[:, 0:1, :]
    q_sub = (q3 * jnp.exp(b3 - r3)).reshape(rows, GLA_DK).astype(BF16)

    ones = jnp.ones((GLA_DK, LANES), BF16)
    half = sb // 2
    t_half = lax.broadcasted_iota(jnp.int32, (nb, half, GLA_DK), 1)
    lane = lax.broadcasted_iota(jnp.int32, (nb, half, LANES), 2)
    sub = lax.broadcasted_iota(jnp.int32, (nb, half, LANES), 0) % (c // sb)
    off_lo = off_hi = lane - sub * sb
    halves = lambda a: (a[:, :half, :], a[:, half:, :])
    (b_lo, b_hi), (q_lo, q_hi) = halves(b3), halves(q3)

    def row_sums(qh, bh, s, first_row):
        e = bh - b3[:, s:s + 1, :]
        if s > first_row:
            e = jnp.where(t_half >= s - first_row, e, NEG_BIG)
        prod = (qh * k3[:, s:s + 1, :] * jnp.exp(e)).reshape(nb * half, GLA_DK).astype(BF16)
        return jnp.dot(prod, ones, preferred_element_type=F32).reshape(nb, half, LANES)

    diag_lo = jnp.zeros((nb, half, LANES), F32)
    diag_hi = jnp.zeros((nb, half, LANES), F32)
    for s in range(sb):
        if s < half:
            diag_lo = jnp.where(off_lo == s, row_sums(q_lo, b_lo, s, 0), diag_lo)
        diag_hi = jnp.where(off_hi == s, row_sums(q_hi, b_hi, s, half), diag_hi)
    diag = jnp.concatenate([diag_lo, diag_hi], axis=1).reshape(rows, LANES)

    kidx = lax.broadcasted_iota(jnp.int32, (c, GLA_DK), 0)
    og = og_ref[...]
    for ci in range(rows // c):
        lo = ci * c
        bc, kc = b[lo:lo + c, :], k[lo:lo + c, :]
        vc = v_ref[lo:lo + c, :].astype(BF16)
        parts = [jnp.zeros((sb, c), F32)]
        for i in range(1, spc):
            r_i = r3[ci * spc + i]
            k_i = (kc * jnp.exp(jnp.where(kidx < i * sb, r_i - bc, NEG_BIG))).astype(BF16)
            parts.append(lax.dot_general(q_sub[lo + i * sb:lo + (i + 1) * sb, :], k_i,
                                         (((1,), (1,)), ((), ())), preferred_element_type=F32))
        scores = jnp.concatenate(parts, axis=0) + diag[lo:lo + c, :c]

        state = state_ref[...]
        o = jnp.dot(scores.astype(BF16), vc, preferred_element_type=F32)
        o = o + lax.dot_general((q[lo:lo + c, :] * jnp.exp(bc)).astype(BF16), state.astype(BF16),
                                (((1,), (1,)), ((), ())), preferred_element_type=F32)
        b_last = bc[c - 1:c, :]
        k_dec = (kc * jnp.exp(b_last - bc)).astype(BF16)
        state_ref[...] = jnp.exp(b_last) * state + lax.dot_general(
            vc, k_dec, (((0,), (0,)), ((), ())), preferred_element_type=F32)

        ms = jnp.mean(o * o, axis=-1, keepdims=True)
        o = o * lax.rsqrt(ms + EPS) * og
        gate = r_ref[lo:lo + c, :]
        o_ref[lo:lo + c, :] = (o * gate * _sigmoid(gate)).astype(o_ref.dtype)


def _gla(p, batch, seq, heads, q_off, k_off, v_off, r_off, lr_off, gate_w_p, gate_b, out_gain):
    p3 = p.reshape(batch, seq, p.shape[1])
    rows = _tile(seq, GLA_ROWS)
    assert rows % GLA_CHUNK == 0
    assert q_off % GLA_DK == 0 and k_off % GLA_DK == 0 and lr_off % LANES == 0
    assert v_off % GLA_DV == 0 and r_off % GLA_DV == 0
    spec = lambda width, blk0: pl.BlockSpec((None, rows, width), lambda b, h, i: (b, i, blk0 + h))
    out = pl.pallas_call(
        _gla_kernel,
        grid=(batch, heads, seq // rows),
        in_specs=[spec(GLA_DK, q_off // GLA_DK), spec(GLA_DK, k_off // GLA_DK),
                  spec(GLA_DV, v_off // GLA_DV), spec(GLA_DV, r_off // GLA_DV),
                  pl.BlockSpec((None, rows, LANES), lambda b, h, i: (b, i, lr_off // LANES)),
                  pl.BlockSpec((LANES, GLA_DK), lambda b, h, i: (0, h)),
                  pl.BlockSpec((1, GLA_DK), lambda b, h, i: (0, h)),
                  pl.BlockSpec((1, GLA_DV), lambda b, h, i: (0, 0))],
        out_specs=pl.BlockSpec((None, rows, GLA_DV), lambda b, h, i: (b, i, h)),
        out_shape=jax.ShapeDtypeStruct((batch, seq, heads * GLA_DV), BF16),
        scratch_shapes=[pltpu.VMEM((GLA_DV, GLA_DK), F32)],
        compiler_params=_params("parallel", "parallel", "arbitrary"),
        name="gla",
    )(p3, p3, p3, p3, p3, gate_w_p, gate_b.reshape(1, -1), out_gain.reshape(1, GLA_DV))
    return out.reshape(batch * seq, heads * GLA_DV)


def _layout(d_model):
    sb_w = 3 * d_model // 8
    conv_w = d_model // 4
    gla_w = d_model - sb_w - conv_w
    gla_heads = gla_w // GLA_DV
    key_w = gla_heads * GLA_DK
    names = ("sb_q", "sb_k", "sb_v", "conv_val", "conv_gate", "gla_q", "gla_k", "gla_v", "gla_r", "gla_lr")
    sizes = (sb_w, sb_w, sb_w, conv_w, conv_w, key_w, key_w, gla_w, gla_w, GLA_RANK)
    src, o = {}, 0
    for nm, sz in zip(names, sizes):
        src[nm] = (o, sz)
        o += sz
    order = ("conv_val", "conv_gate", "gla_v", "gla_r", "gla_q", "gla_k", "gla_lr")
    dst, o = {}, 0
    for nm in order:
        dst[nm] = o
        o += src[nm][1]
    o = dst["gla_lr"] + INPROJ_TN
    return dict(sb_w=sb_w, conv_w=conv_w, gla_w=gla_w, sb_heads=sb_w // HEAD_DIM,
                gla_heads=gla_heads, src=src, dst=dst, p_w=o, in_w=sum(sizes))


def kernel(x, ffn1_norm, ffn1_w_in, ffn1_w_out, mix_norm, mix_w_in, sb_q_gain, sb_k_gain, conv_w, conv_b, conv_ln_g, conv_ln_b, gla_gate_w, gla_gate_b, gla_out_gain, mix_w_out, ffn2_norm, ffn2_w_in, ffn2_w_out, final_norm):
    batch, seq, d = x.shape
    depth = ffn1_norm.shape[0]
    lay = _layout(d)
    dst = lay["dst"]
    assert mix_w_in.shape[2] == lay["in_w"]

    ffn1_w_out, ffn2_w_out, mix_w_out = (w.astype(BF16) for w in (ffn1_w_out, ffn2_w_out, mix_w_out))
    mix_w_in = jnp.swapaxes(mix_w_in, 1, 2)
    x2 = x.reshape(batch * seq, d)
    xg, ss = _norm(x2, None, ffn1_norm[0])
    for l in range(depth):
        x2, xg, ss = _ffn(x2, xg, ss, ffn1_w_in, ffn1_w_out, l, mix_norm[l])

        p, qkv = _in_proj(xg, ss, mix_w_in, l, lay, sb_q_gain[l], sb_k_gain[l])
        ya = _sb_attention(qkv, batch, seq, lay["sb_heads"])
        yb = _conv_module(p, batch, seq, dst["conv_val"], dst["conv_gate"], lay["conv_w"],
                          conv_w[l], conv_b[l], conv_ln_g[l], conv_ln_b[l])
        gate_w_p = jnp.pad(gla_gate_w[l], ((0, LANES - GLA_RANK), (0, 0)))
        yc = _gla(p, batch, seq, lay["gla_heads"], dst["gla_q"], dst["gla_k"], dst["gla_v"],
                  dst["gla_r"], dst["gla_lr"], gate_w_p, gla_gate_b[l], gla_out_gain[l])
        x2, xg, ss = _out_proj(ya, yb, yc, mix_w_out, l, x2, ffn2_norm[l])

        (x2,) = _ffn(x2, xg, ss, ffn2_w_in, ffn2_w_out, l, None)
        if l + 1 < depth:
            x2, xg, ss = _norm(x2, final_norm[l], ffn1_norm[l + 1])
        else:
            (x2,) = _norm(x2, final_norm[l], None)
    return x2.reshape(batch, seq, d)
```

```python
import functools

import jax
import jax.numpy as jnp
import numpy as np
from jax import lax
from jax.experimental import pallas as pl
from jax.experimental.pallas import tpu as pltpu

F32 = jnp.float32
BF16 = jnp.bfloat16

HEAD_DIM = 128
GLA_DK = 128
GLA_DV = 256
GLA_RANK = 16
GLA_TAU = 16.0
GLA_CHUNK = 64
CONV_K = 31
EPS = 1e-6
SB_SCALE = HEAD_DIM ** -0.5
GLA_SCALE = GLA_DK ** -0.5

LANES = 128
SUBLANES = 8
VMEM_LIMIT_BYTES = 56 * 1024 * 1024

NEG_BIG = -1e30


def _params(*sem):
    return pltpu.CompilerParams(dimension_semantics=sem, vmem_limit_bytes=VMEM_LIMIT_BYTES)


def _tile(n, want):
    if n <= want:
        return n
    t = want
    while n % t:
        t -= SUBLANES
    return t


def _split_bf16(x, terms):
    out = []
    r = x
    for _ in range(terms):
        h = r.astype(BF16)
        out.append(h)
        r = r - h.astype(F32)
    return out


def _softplus(z):
    return jnp.maximum(z, 0.0) + jnp.log(1.0 + jnp.exp(-jnp.abs(z)))


def _sigmoid(z):
    return 1.0 / (1.0 + jnp.exp(-z))


def _row_scale(ss_ref, d):
    return lax.rsqrt(ss_ref[:, 0:1] * (1.0 / d) + EPS)


def _store_residual(y, g_ref, o_ref, xg_ref, ss_ref):
    o_ref[...] = y
    if xg_ref is None:
        return
    xg_ref[...] = (y * g_ref[...]).astype(xg_ref.dtype)
    part = jnp.broadcast_to(jnp.sum(y * y, axis=-1, keepdims=True), ss_ref.shape)
    j = pl.program_id(1)

    @pl.when(j == 0)
    def _():
        ss_ref[...] = part

    @pl.when(j > 0)
    def _():
        ss_ref[...] += part


def _residual_out(n, d, tm, tn, with_norm):
    specs = [pl.BlockSpec((tm, tn), lambda i, j: (i, j))]
    shapes = [jax.ShapeDtypeStruct((n, d), F32)]
    if with_norm:
        specs += [pl.BlockSpec((tm, tn), lambda i, j: (i, j)), pl.BlockSpec((tm, LANES), lambda i, j: (i, 0))]
        shapes += [jax.ShapeDtypeStruct((n, d), BF16), jax.ShapeDtypeStruct((n, LANES), F32)]
    return specs, shapes


def _norm_kernel(*refs, d, with_out, with_pair):
    refs = list(refs)
    x_ref = refs.pop(0)
    x = x_ref[...]
    if with_out:
        g_ref = refs.pop(0)
    if with_pair:
        gn_ref = refs.pop(0)
    if with_out:
        o_ref = refs.pop(0)
        ms = jnp.mean(x * x, axis=-1, keepdims=True)
        x = x * lax.rsqrt(ms + EPS) * g_ref[...]
        o_ref[...] = x
    if with_pair:
        xg_ref, ss_ref = refs
        xg_ref[...] = (x * gn_ref[...]).astype(xg_ref.dtype)
        ss_ref[...] = jnp.broadcast_to(jnp.sum(x * x, axis=-1, keepdims=True), ss_ref.shape)


def _norm(x2, g_out, g_pair):
    n, d = x2.shape
    tm = _tile(n, 256)
    row = pl.BlockSpec((tm, d), lambda i: (i, 0))
    vec = pl.BlockSpec((1, d), lambda i: (0, 0))
    ins, in_specs, out_specs, out_shape = [x2], [row], [], []
    if g_out is not None:
        ins.append(g_out.reshape(1, d)); in_specs.append(vec)
        out_specs.append(row); out_shape.append(jax.ShapeDtypeStruct((n, d), F32))
    if g_pair is not None:
        ins.append(g_pair.reshape(1, d)); in_specs.append(vec)
        out_specs += [row, pl.BlockSpec((tm, LANES), lambda i: (i, 0))]
        out_shape += [jax.ShapeDtypeStruct((n, d), BF16), jax.ShapeDtypeStruct((n, LANES), F32)]
    return pl.pallas_call(
        functools.partial(_norm_kernel, d=d, with_out=g_out is not None, with_pair=g_pair is not None),
        grid=(n // tm,),
        in_specs=in_specs, out_specs=out_specs, out_shape=out_shape,
        compiler_params=_params("parallel"),
        name="rmsnorm",
    )(*ins)


FFN_TF = 256
FFN_ROWS = 512


def _ffn_up_kernel(xg_ref, ss_ref, wg_ref, wu_ref, o_ref):
    wg = wg_ref[...].astype(BF16)
    wu = wu_ref[...].astype(BF16)
    d = xg_ref.shape[1]
    tm = xg_ref.shape[0]
    chunk = FFN_ROWS if tm % FFN_ROWS == 0 else tm
    for r0 in range(0, tm, chunk):
        rows = slice(r0, r0 + chunk)
        xg = xg_ref[rows, :]
        s = lax.rsqrt(ss_ref[rows, 0:1] * (1.0 / d) + EPS)
        g = s * jnp.dot(xg, wg, preferred_element_type=F32)
        u = s * jnp.dot(xg, wu, preferred_element_type=F32)
        o_ref[rows, :] = (g * _sigmoid(g) * u).astype(o_ref.dtype)


def _ffn_down_kernel(a_ref, w_ref, x_ref, *rest):
    g_ref, o_ref, xg_ref, ss_ref = rest if len(rest) == 4 else (None, rest[0], None, None)
    acc = jnp.dot(a_ref[...], w_ref[...], preferred_element_type=F32)
    _store_residual(x_ref[...] + 0.5 * acc, g_ref, o_ref, xg_ref, ss_ref)


def _ffn(x2, xg, ss, w_in, w_out, layer, g_next):
    n, d = x2.shape
    f = w_out.shape[1]
    assert w_in.shape[1:] == (d, 2 * f) and f % FFN_TF == 0
    nj = f // FFN_TF
    tm = _tile(n, 2048)
    a = pl.pallas_call(
        _ffn_up_kernel,
        grid=(n // tm, nj),
        in_specs=[pl.BlockSpec((tm, d), lambda i, j: (i, 0), pipeline_mode=pl.Buffered(1)),
                  pl.BlockSpec((tm, LANES), lambda i, j: (i, 0)),
                  pl.BlockSpec((None, d, FFN_TF), lambda i, j: (layer, 0, j)),
                  pl.BlockSpec((None, d, FFN_TF), lambda i, j: (layer, 0, nj + j))],
        out_specs=pl.BlockSpec((tm, FFN_TF), lambda i, j: (i, j)),
        out_shape=jax.ShapeDtypeStruct((n, f), BF16),
        compiler_params=_params("parallel", "arbitrary"),
        name="ffn_up",
    )(xg, ss, w_in, w_in)
    tm2 = _tile(n, 512)
    tn = _tile(d, 512)
    with_norm = g_next is not None
    out_specs, out_shape = _residual_out(n, d, tm2, tn, with_norm)
    ins = [a, w_out, x2] + ([g_next.reshape(1, d)] if with_norm else [])
    in_specs = [pl.BlockSpec((tm2, f), lambda i, j: (i, 0)),
                pl.BlockSpec((None, f, tn), lambda i, j: (layer, 0, j)),
                pl.BlockSpec((tm2, tn), lambda i, j: (i, j))]
    if with_norm:
        in_specs.append(pl.BlockSpec((1, tn), lambda i, j: (0, j)))
    return pl.pallas_call(
        _ffn_down_kernel,
        grid=(n // tm2, d // tn),
        in_specs=in_specs, out_specs=out_specs, out_shape=out_shape,
        compiler_params=_params("parallel", "arbitrary"),
        name="ffn_down",
    )(*ins)


INPROJ_TN = 512


def _in_proj_kernel(pt_ref, qt_ref, xg_ref, ss_ref, w_ref, qg_ref, kg_ref, p_ref, qkv_ref, *,
                    sb_tiles, last_tile, last_valid):
    del pt_ref, qt_ref
    j = pl.program_id(1)
    xg = xg_ref[...]
    acc = _row_scale(ss_ref, xg.shape[1]) * lax.dot_general(
        xg, w_ref[...].astype(BF16), (((1,), (1,)), ((), ())), preferred_element_type=F32)

    def head_norm(gain):
        for c0 in range(0, INPROJ_TN, HEAD_DIM):
            x = acc[:, c0:c0 + HEAD_DIM]
            ms = jnp.mean(x * x, axis=-1, keepdims=True)
            qkv_ref[:, c0:c0 + HEAD_DIM] = (x * lax.rsqrt(ms + EPS) * gain).astype(qkv_ref.dtype)

    @pl.when(j < sb_tiles)
    def _():
        head_norm(qg_ref[...] * SB_SCALE)

    @pl.when(jnp.logical_and(j >= sb_tiles, j < 2 * sb_tiles))
    def _():
        head_norm(kg_ref[...])

    @pl.when(jnp.logical_and(j >= 2 * sb_tiles, j < 3 * sb_tiles))
    def _():
        qkv_ref[...] = acc.astype(qkv_ref.dtype)

    @pl.when(jnp.logical_and(j >= 3 * sb_tiles, j < last_tile))
    def _():
        p_ref[...] = acc

    @pl.when(j == last_tile)
    def _():
        lane = lax.broadcasted_iota(jnp.int32, acc.shape, 1)
        p_ref[...] = jnp.where(lane < last_valid, acc, 0.0)


def _in_proj(xg, ss, w_in, layer, lay, q_gain, k_gain):
    n, d = xg.shape
    src, dst = lay["src"], lay["dst"]
    tn = INPROJ_TN
    nt = -(-lay["in_w"] // tn)
    sb_tiles = lay["sb_w"] // tn
    assert lay["sb_w"] % tn == 0 and src["sb_q"][0] == 0 and src["sb_k"][0] == lay["sb_w"]
    col_dst = np.full(nt * tn, -1)
    for nm in dst:
        o, sz = src[nm]
        col_dst[o:o + sz] = dst[nm] + np.arange(sz)
    p_tile = []
    for t in range(nt):
        cols = col_dst[t * tn:(t + 1) * tn]
        real = np.flatnonzero(cols >= 0)
        first = int(cols[real[0]] - real[0]) if real.size else 0
        assert first % tn == 0 and (cols[real] == first + real).all() and real.size in (0, tn, lay["in_w"] % tn)
        p_tile.append(first // tn)
    q_tile = [min(t, 3 * sb_tiles - 1) for t in range(nt)]
    tm = _tile(n, 2048)
    kern = functools.partial(_in_proj_kernel, sb_tiles=sb_tiles, last_tile=nt - 1,
                             last_valid=lay["in_w"] - (nt - 1) * tn)
    grid_spec = pltpu.PrefetchScalarGridSpec(
        num_scalar_prefetch=2,
        grid=(n // tm, nt),
        in_specs=[pl.BlockSpec((tm, d), lambda i, j, pt, qt: (i, 0), pipeline_mode=pl.Buffered(1)),
                  pl.BlockSpec((tm, LANES), lambda i, j, pt, qt: (i, 0)),
                  pl.BlockSpec((None, tn, d), lambda i, j, pt, qt: (layer, j, 0)),
                  pl.BlockSpec((1, HEAD_DIM), lambda i, j, pt, qt: (0, 0)),
                  pl.BlockSpec((1, HEAD_DIM), lambda i, j, pt, qt: (0, 0))],
        out_specs=[pl.BlockSpec((tm, tn), lambda i, j, pt, qt: (i, pt[j])),
                   pl.BlockSpec((tm, tn), lambda i, j, pt, qt: (i, qt[j]))])
    return pl.pallas_call(
        kern,
        grid_spec=grid_spec,
        out_shape=[jax.ShapeDtypeStruct((n, lay["p_w"]), F32),
                   jax.ShapeDtypeStruct((n, 3 * lay["sb_w"]), BF16)],
        compiler_params=_params("parallel", "arbitrary"),
        name="mix_in_proj",
    )(jnp.asarray(p_tile, jnp.int32), jnp.asarray(q_tile, jnp.int32), xg, ss, w_in,
      q_gain.reshape(1, HEAD_DIM), k_gain.reshape(1, HEAD_DIM))


def _out_proj_kernel(ya_ref, yb_ref, yc_ref, w_ref, x_ref, g_ref, o_ref, xg_ref, ss_ref, y_ref):
    ka, kb = ya_ref.shape[1], yb_ref.shape[1]

    @pl.when(pl.program_id(1) == 0)
    def _():
        y_ref[:, 0:ka] = ya_ref[...]
        y_ref[:, ka:ka + kb] = yb_ref[...]
        y_ref[:, ka + kb:] = yc_ref[...]

    y = x_ref[...] + jnp.dot(y_ref[...], w_ref[...], preferred_element_type=F32)
    _store_residual(y, g_ref, o_ref, xg_ref, ss_ref)


def _out_proj(ya, yb, yc, w, layer, x2, g_next):
    n, d = x2.shape
    tm = _tile(n, 1024)
    tn = _tile(d, 512)
    ka, kb, kc = ya.shape[1], yb.shape[1], yc.shape[1]
    kw = w.shape[1]
    assert ka + kb + kc == kw
    out_specs, out_shape = _residual_out(n, d, tm, tn, True)
    return pl.pallas_call(
        _out_proj_kernel,
        grid=(n // tm, d // tn),
        in_specs=[pl.BlockSpec((tm, ka), lambda i, j: (i, 0)),
                  pl.BlockSpec((tm, kb), lambda i, j: (i, 0)),
                  pl.BlockSpec((tm, kc), lambda i, j: (i, 0)),
                  pl.BlockSpec((None, kw, tn), lambda i, j: (layer, 0, j)),
                  pl.BlockSpec((tm, tn), lambda i, j: (i, j)),
                  pl.BlockSpec((1, tn), lambda i, j: (0, j))],
        out_specs=out_specs, out_shape=out_shape,
        scratch_shapes=[pltpu.VMEM((tm, kw), BF16)],
        compiler_params=_params("parallel", "arbitrary"),
        name="mix_out_proj",
    )(ya, yb, yc, w, x2, g_next.reshape(1, d))


SB_T = 256
SB_HEADS_PER_STEP = 4
SB_DEAD = 105.0


def _sb_attn_kernel(q_ref, k_ref, v_ref, o_ref):
    t = SB_T
    i = pl.program_id(2)
    heads = range(q_ref.shape[1] // HEAD_DIM)
    row = lax.broadcasted_iota(jnp.int32, (t, t), 0)
    col = lax.broadcasted_iota(jnp.int32, (t, t), 1)
    tri = jnp.where(row >= col, 1.0, 0.0).astype(BF16)

    def block(h, kb, carry, limit):
        acc, run = carry
        cols = slice(h * HEAD_DIM, (h + 1) * HEAD_DIM)
        start = pl.multiple_of(kb * t, t)
        k = k_ref[pl.ds(start, t), cols]
        v = v_ref[pl.ds(start, t), cols]
        z = lax.dot_general(q_ref[:, cols], k, (((1,), (1,)), ((), ())), preferred_element_type=F32)
        sp = _softplus(z)
        if limit is not None:
            keep = col < limit
            sp = jnp.where(keep, sp, 0.0)
        hi, lo = _split_bf16(sp, 2)
        c = (jnp.dot(hi, tri, preferred_element_type=F32)
             + jnp.dot(lo, tri, preferred_element_type=F32))
        e = z - c - run
        if limit is not None:
            e = jnp.where(keep, e, NEG_BIG)
        a = jnp.exp(e)
        acc = acc + jnp.dot(a.astype(BF16), v, preferred_element_type=F32)
        return acc, run + c[:, :1]

    carries = [(jnp.zeros((t, HEAD_DIM), F32), jnp.zeros((t, 1), F32)) for _ in heads]
    carries = [block(h, i, carries[h], row) for h in heads]
    carries = [block(h, jnp.maximum(i - 1, 0), carries[h], jnp.where(i >= 1, t, 0)) for h in heads]

    def run_min(carries):
        return functools.reduce(jnp.minimum, [jnp.min(run) for _, run in carries])

    def live(st):
        kb, _, low = st
        return jnp.logical_and(kb >= 0, low < SB_DEAD)

    def step(st):
        kb, carries, _ = st
        carries = [block(h, kb, carries[h], None) for h in heads]
        return kb - 1, carries, run_min(carries)

    _, carries, _ = lax.while_loop(live, step, (i - 2, carries, run_min(carries)))
    for h in heads:
        o_ref[:, h * HEAD_DIM:(h + 1) * HEAD_DIM] = carries[h][0].astype(o_ref.dtype)


def _sb_attention(qkv, batch, seq, heads):
    w = heads * HEAD_DIM
    q3 = qkv.reshape(batch, seq, 3 * w)
    k3 = v3 = q3
    hps = SB_HEADS_PER_STEP if heads % SB_HEADS_PER_STEP == 0 else 1
    groups, gw = heads // hps, hps * HEAD_DIM
    assert seq % SB_T == 0
    out = pl.pallas_call(
        _sb_attn_kernel,
        grid=(batch, groups, seq // SB_T),
        in_specs=[pl.BlockSpec((None, SB_T, gw), lambda b, g, i: (b, i, g)),
                  pl.BlockSpec((None, seq, gw), lambda b, g, i: (b, 0, groups + g)),
                  pl.BlockSpec((None, seq, gw), lambda b, g, i: (b, 0, 2 * groups + g))],
        out_specs=pl.BlockSpec((None, SB_T, gw), lambda b, g, i: (b, i, g)),
        out_shape=jax.ShapeDtypeStruct((batch, seq, w), BF16),
        compiler_params=_params("parallel", "parallel", "arbitrary"),
        name="sb_attention",
    )(q3, k3, v3)
    return out.reshape(batch * seq, w)


CONV_TS = 512
CONV_HALO = 32
CONV_ROWS = 32


def _conv_kernel(val_ref, gate_ref, hval_ref, hgate_ref, w_ref, b_ref, lg_ref, lb_ref, o_ref,
                 u_ref, us_ref):
    ts = o_ref.shape[0]
    i = pl.program_id(1)
    halo = hval_ref[...] * _sigmoid(hgate_ref[...])
    u_ref[0:CONV_HALO, :] = jnp.where(i > 0, halo, 0.0)
    u_ref[CONV_HALO:, :] = val_ref[...] * _sigmoid(gate_ref[...])
    span = CONV_HALO + ts - SUBLANES
    for m in range(1, SUBLANES):
        us_ref[m - 1, 0:span, :] = u_ref[m:m + span, :]
    base = CONV_HALO - (CONV_K - 1)
    bias, lg, lb = b_ref[...], lg_ref[...], lb_ref[...]
    width = o_ref.shape[1]
    for r0 in range(0, ts, CONV_ROWS):
        acc = jnp.zeros((CONV_ROWS // SUBLANES, SUBLANES, width), F32)
        for k in range(CONV_K):
            vreg_row, m = divmod(base + k, SUBLANES)
            lo = vreg_row * SUBLANES + r0
            src = u_ref[lo:lo + CONV_ROWS, :] if m == 0 else us_ref[m - 1, lo:lo + CONV_ROWS, :]
            acc = acc + w_ref[k] * src.reshape(acc.shape)
        acc = acc.reshape(CONV_ROWS, width) + bias
        mu = jnp.mean(acc, axis=-1, keepdims=True)
        cen = acc - mu
        var = jnp.mean(cen * cen, axis=-1, keepdims=True)
        y = cen * lax.rsqrt(var + EPS) * lg + lb
        o_ref[r0:r0 + CONV_ROWS, :] = (y * _sigmoid(y)).astype(o_ref.dtype)


def _conv_module(p, batch, seq, val_off, gate_off, width, conv_w, conv_b, ln_g, ln_b):
    p3 = p.reshape(batch, seq, p.shape[1])
    ts = _tile(seq, CONV_TS)
    assert val_off % width == 0 and gate_off % width == 0 and ts % CONV_HALO == 0
    cv, cg = val_off // width, gate_off // width
    hpt = ts // CONV_HALO
    main = lambda c: pl.BlockSpec((None, ts, width), lambda b, i: (b, i, c))
    halo = lambda c: pl.BlockSpec((None, CONV_HALO, width),
                                  lambda b, i: (b, jnp.maximum(i * hpt - 1, 0), c))
    vec = lambda r: pl.BlockSpec((r, width), lambda b, i: (0, 0))
    out = pl.pallas_call(
        _conv_kernel,
        grid=(batch, seq // ts),
        in_specs=[main(cv), main(cg), halo(cv), halo(cg),
                  pl.BlockSpec((CONV_K, SUBLANES, width), lambda b, i: (0, 0, 0)), vec(1), vec(1), vec(1)],
        out_specs=pl.BlockSpec((None, ts, width), lambda b, i: (b, i, 0)),
        out_shape=jax.ShapeDtypeStruct((batch, seq, width), BF16),
        scratch_shapes=[pltpu.VMEM((CONV_HALO + ts, width), F32),
                        pltpu.VMEM((SUBLANES - 1, CONV_HALO + ts, width), F32)],
        compiler_params=_params("parallel", "arbitrary"),
        name="conv_module",
    )(p3, p3, p3, p3, jnp.broadcast_to(conv_w[:, None, :], (CONV_K, SUBLANES, width)),
      conv_b.reshape(1, width), ln_g.reshape(1, width), ln_b.reshape(1, width))
    return out.reshape(batch * seq, width)


GLA_ROWS = 512
GLA_SUB = 16


def _gla_kernel(q_ref, k_ref, v_ref, r_ref, lr_ref, gw_ref, gb_ref, og_ref, o_ref, state_ref):
    c, sb = GLA_CHUNK, GLA_SUB
    rows = q_ref.shape[0]
    nb, spc = rows // sb, c // sb

    @pl.when(pl.program_id(2) == 0)
    def _():
        state_ref[...] = jnp.zeros_like(state_ref)

    lr_parts = _split_bf16(lr_ref[...], 2)
    gw_parts = _split_bf16(gw_ref[...], 2)
    logit = (jnp.dot(lr_parts[0], gw_parts[0], preferred_element_type=F32)
             + jnp.dot(lr_parts[1], gw_parts[0], preferred_element_type=F32)
             + jnp.dot(lr_parts[0], gw_parts[1], preferred_element_type=F32)) + gb_ref[...]
    g = -_softplus(-logit) * (1.0 / GLA_TAU)

    row = lax.broadcasted_iota(jnp.int32, (c, c), 0)
    col = lax.broadcasted_iota(jnp.int32, (c, c), 1)
    tril = jnp.where(row >= col, 1.0, 0.0).astype(BF16)
    g_parts = _split_bf16(g, 3)
    b = jnp.concatenate(
        [sum(jnp.dot(tril, part[r0:r0 + c, :], preferred_element_type=F32) for part in g_parts)
         for r0 in range(0, rows, c)], axis=0)

    q = q_ref[...] * GLA_SCALE
    k = k_ref[...]
    b3, q3, k3 = (a.reshape(nb, sb, GLA_DK) for a in (b, q, k))
    r3 = (b3 - g.reshape(nb, sb, GLA_DK))[:, 0:1, :]
    q_sub = (q3 * jnp.exp(b3 - r3)).reshape(rows, GLA_DK).astype(BF16)

    ones = jnp.ones((GLA_DK, LANES), BF16)
    half = sb // 2
    t_half = lax.broadcasted_iota(jnp.int32, (nb, half, GLA_DK), 1)
    lane = lax.broadcasted_iota(jnp.int32, (nb, half, LANES), 2)
    sub = lax.broadcasted_iota(jnp.int32, (nb, half, LANES), 0) % (c // sb)
    off_lo = off_hi = lane - sub * sb
    halves = lambda a: (a[:, :half, :], a[:, half:, :])
    (b_lo, b_hi), (q_lo, q_hi) = halves(b3), halves(q3)

    def row_sums(qh, bh, s, first_row):
        e = bh - b3[:, s:s + 1, :]
        if s > first_row:
            e = jnp.where(t_half >= s - first_row, e, NEG_BIG)
        prod = (qh * k3[:, s:s + 1, :] * jnp.exp(e)).reshape(nb * half, GLA_DK).astype(BF16)
        return jnp.dot(prod, ones, preferred_element_type=F32).reshape(nb, half, LANES)

    diag_lo = jnp.zeros((nb, half, LANES), F32)
    diag_hi = jnp.zeros((nb, half, LANES), F32)
    for s in range(sb):
        if s < half:
            diag_lo = jnp.where(off_lo == s, row_sums(q_lo, b_lo, s, 0), diag_lo)
        diag_hi = jnp.where(off_hi == s, row_sums(q_hi, b_hi, s, half), diag_hi)
    diag = jnp.concatenate([diag_lo, diag_hi], axis=1).reshape(rows, LANES)

    kidx = lax.broadcasted_iota(jnp.int32, (c, GLA_DK), 0)
    og = og_ref[...]
    for ci in range(rows // c):
        lo = ci * c
        bc, kc = b[lo:lo + c, :], k[lo:lo + c, :]
        vc = v_ref[lo:lo + c, :].astype(BF16)
        parts = [jnp.zeros((sb, c), F32)]
        for i in range(1, spc):
            r_i = r3[ci * spc + i]
            k_i = (kc * jnp.exp(jnp.where(kidx < i * sb, r_i - bc, NEG_BIG))).astype(BF16)
            parts.append(lax.dot_general(q_sub[lo + i * sb:lo + (i + 1) * sb, :], k_i,
                                         (((1,), (1,)), ((), ())), preferred_element_type=F32))
        scores = jnp.concatenate(parts, axis=0) + diag[lo:lo + c, :c]

        state = state_ref[...]
        o = jnp.dot(scores.astype(BF16), vc, preferred_element_type=F32)
        o = o + lax.dot_general((q[lo:lo + c, :] * jnp.exp(bc)).astype(BF16), state.astype(BF16),
                                (((1,), (1,)), ((), ())), preferred_element_type=F32)
        b_last = bc[c - 1:c, :]
        k_dec = (kc * jnp.exp(b_last - bc)).astype(BF16)
        state_ref[...] = jnp.exp(b_last) * state + lax.dot_general(
            vc, k_dec, (((0,), (0,)), ((), ())), preferred_element_type=F32)

        ms = jnp.mean(o * o, axis=-1, keepdims=True)
        o = o * lax.rsqrt(ms + EPS) * og
        gate = r_ref[lo:lo + c, :]
        o_ref[lo:lo + c, :] = (o * gate * _sigmoid(gate)).astype(o_ref.dtype)


def _gla(p, batch, seq, heads, q_off, k_off, v_off, r_off, lr_off, gate_w_p, gate_b, out_gain):
    p3 = p.reshape(batch, seq, p.shape[1])
    rows = _tile(seq, GLA_ROWS)
    assert rows % GLA_CHUNK == 0
    assert q_off % GLA_DK == 0 and k_off % GLA_DK == 0 and lr_off % LANES == 0
    assert v_off % GLA_DV == 0 and r_off % GLA_DV == 0
    spec = lambda width, blk0: pl.BlockSpec((None, rows, width), lambda b, h, i: (b, i, blk0 + h))
    out = pl.pallas_call(
        _gla_kernel,
        grid=(batch, heads, seq // rows),
        in_specs=[spec(GLA_DK, q_off // GLA_DK), spec(GLA_DK, k_off // GLA_DK),
                  spec(GLA_DV, v_off // GLA_DV), spec(GLA_DV, r_off // GLA_DV),
                  pl.BlockSpec((None, rows, LANES), lambda b, h, i: (b, i, lr_off // LANES)),
                  pl.BlockSpec((LANES, GLA_DK), lambda b, h, i: (0, h)),
                  pl.BlockSpec((1, GLA_DK), lambda b, h, i: (0, h)),
                  pl.BlockSpec((1, GLA_DV), lambda b, h, i: (0, 0))],
        out_specs=pl.BlockSpec((None, rows, GLA_DV), lambda b, h, i: (b, i, h)),
        out_shape=jax.ShapeDtypeStruct((batch, seq, heads * GLA_DV), BF16),
        scratch_shapes=[pltpu.VMEM((GLA_DV, GLA_DK), F32)],
        compiler_params=_params("parallel", "parallel", "arbitrary"),
        name="gla",
    )(p3, p3, p3, p3, p3, gate_w_p, gate_b.reshape(1, -1), out_gain.reshape(1, GLA_DV))
    return out.reshape(batch * seq, heads * GLA_DV)


def _layout(d_model):
    sb_w = 3 * d_model // 8
    conv_w = d_model // 4
    gla_w = d_model - sb_w - conv_w
    gla_heads = gla_w // GLA_DV
    key_w = gla_heads * GLA_DK
    names = ("sb_q", "sb_k", "sb_v", "conv_val", "conv_gate", "gla_q", "gla_k", "gla_v", "gla_r", "gla_lr")
    sizes = (sb_w, sb_w, sb_w, conv_w, conv_w, key_w, key_w, gla_w, gla_w, GLA_RANK)
    src, o = {}, 0
    for nm, sz in zip(names, sizes):
        src[nm] = (o, sz)
        o += sz
    order = ("conv_val", "conv_gate", "gla_v", "gla_r", "gla_q", "gla_k", "gla_lr")
    dst, o = {}, 0
    for nm in order:
        dst[nm] = o
        o += src[nm][1]
    o = dst["gla_lr"] + INPROJ_TN
    return dict(sb_w=sb_w, conv_w=conv_w, gla_w=gla_w, sb_heads=sb_w // HEAD_DIM,
                gla_heads=gla_heads, src=src, dst=dst, p_w=o, in_w=sum(sizes))


def kernel(x, ffn1_norm, ffn1_w_in, ffn1_w_out, mix_norm, mix_w_in, sb_q_gain, sb_k_gain, conv_w, conv_b, conv_ln_g, conv_ln_b, gla_gate_w, gla_gate_b, gla_out_gain, mix_w_out, ffn2_norm, ffn2_w_in, ffn2_w_out, final_norm):
    batch, seq, d = x.shape
    depth = ffn1_norm.shape[0]
    lay = _layout(d)
    dst = lay["dst"]
    assert mix_w_in.shape[2] == lay["in_w"]

    ffn1_w_out, ffn2_w_out, mix_w_out = (w.astype(BF16) for w in (ffn1_w_out, ffn2_w_out, mix_w_out))
    mix_w_in = jnp.swapaxes(mix_w_in, 1, 2)
    x2 = x.reshape(batch * seq, d)
    xg, ss = _norm(x2, None, ffn1_norm[0])
    for l in range(depth):
        x2, xg, ss = _ffn(x2, xg, ss, ffn1_w_in, ffn1_w_out, l, mix_norm[l])

        p, qkv = _in_proj(xg, ss, mix_w_in, l, lay, sb_q_gain[l], sb_k_gain[l])
        ya = _sb_attention(qkv, batch, seq, lay["sb_heads"])
        yb = _conv_module(p, batch, seq, dst["conv_val"], dst["conv_gate"], lay["conv_w"],
                          conv_w[l], conv_b[l], conv_ln_g[l], conv_ln_b[l])
        gate_w_p = jnp.pad(gla_gate_w[l], ((0, LANES - GLA_RANK), (0, 0)))
        yc = _gla(p, batch, seq, lay["gla_heads"], dst["gla_q"], dst["gla_k"], dst["gla_v"],
                  dst["gla_r"], dst["gla_lr"], gate_w_p, gla_gate_b[l], gla_out_gain[l])
        x2, xg, ss = _out_proj(ya, yb, yc, mix_w_out, l, x2, ffn2_norm[l])

        (x2,) = _ffn(x2, xg, ss, ffn2_w_in, ffn2_w_out, l, None)
        if l + 1 < depth:
            x2, xg, ss = _norm(x2, final_norm[l], ffn1_norm[l + 1])
        else:
            (x2,) = _norm(x2, final_norm[l], None)
    return x2.reshape(batch, seq, d)
```

```python
import functools

import jax
import jax.numpy as jnp
import numpy as np
from jax import lax
from jax.experimental import pallas as pl
from jax.experimental.pallas import tpu as pltpu

F32 = jnp.float32
BF16 = jnp.bfloat16

HEAD_DIM = 128
GLA_DK = 128
GLA_DV = 256
GLA_RANK = 16
GLA_TAU = 16.0
GLA_CHUNK = 64
CONV_K = 31
EPS = 1e-6
SB_SCALE = HEAD_DIM ** -0.5
GLA_SCALE = GLA_DK ** -0.5

LANES = 128
SUBLANES = 8
VMEM_LIMIT_BYTES = 56 * 1024 * 1024

NEG_BIG = -1e30


def _params(*sem):
    return pltpu.CompilerParams(dimension_semantics=sem, vmem_limit_bytes=VMEM_LIMIT_BYTES)


def _tile(n, want):
    if n <= want:
        return n
    t = want
    while n % t:
        t -= SUBLANES
    return t


def _split_bf16(x, terms):
    out = []
    r = x
    for _ in range(terms):
        h = r.astype(BF16)
        out.append(h)
        r = r - h.astype(F32)
    return out


def _softplus(z):
    return jnp.maximum(z, 0.0) + jnp.log(1.0 + jnp.exp(-jnp.abs(z)))


def _sigmoid(z):
    return 1.0 / (1.0 + jnp.exp(-z))


def _row_scale(ss_ref, d):
    return lax.rsqrt(ss_ref[:, 0:1] * (1.0 / d) + EPS)


def _store_residual(y, g_ref, o_ref, xg_ref, ss_ref):
    o_ref[...] = y
    if xg_ref is None:
        return
    xg_ref[...] = (y * g_ref[...]).astype(xg_ref.dtype)
    part = jnp.broadcast_to(jnp.sum(y * y, axis=-1, keepdims=True), ss_ref.shape)
    j = pl.program_id(1)

    @pl.when(j == 0)
    def _():
        ss_ref[...] = part

    @pl.when(j > 0)
    def _():
        ss_ref[...] += part


def _residual_out(n, d, tm, tn, with_norm):
    specs = [pl.BlockSpec((tm, tn), lambda i, j: (i, j))]
    shapes = [jax.ShapeDtypeStruct((n, d), F32)]
    if with_norm:
        specs += [pl.BlockSpec((tm, tn), lambda i, j: (i, j)), pl.BlockSpec((tm, LANES), lambda i, j: (i, 0))]
        shapes += [jax.ShapeDtypeStruct((n, d), BF16), jax.ShapeDtypeStruct((n, LANES), F32)]
    return specs, shapes


def _norm_kernel(*refs, d, with_out, with_pair):
    refs = list(refs)
    x_ref = refs.pop(0)
    x = x_ref[...]
    if with_out:
        g_ref = refs.pop(0)
    if with_pair:
        gn_ref = refs.pop(0)
    if with_out:
        o_ref = refs.pop(0)
        ms = jnp.mean(x * x, axis=-1, keepdims=True)
        x = x * lax.rsqrt(ms + EPS) * g_ref[...]
        o_ref[...] = x
    if with_pair:
        xg_ref, ss_ref = refs
        xg_ref[...] = (x * gn_ref[...]).astype(xg_ref.dtype)
        ss_ref[...] = jnp.broadcast_to(jnp.sum(x * x, axis=-1, keepdims=True), ss_ref.shape)


def _norm(x2, g_out, g_pair):
    n, d = x2.shape
    tm = _tile(n, 256)
    row = pl.BlockSpec((tm, d), lambda i: (i, 0))
    vec = pl.BlockSpec((1, d), lambda i: (0, 0))
    ins, in_specs, out_specs, out_shape = [x2], [row], [], []
    if g_out is not None:
        ins.append(g_out.reshape(1, d)); in_specs.append(vec)
        out_specs.append(row); out_shape.append(jax.ShapeDtypeStruct((n, d), F32))
    if g_pair is not None:
        ins.append(g_pair.reshape(1, d)); in_specs.append(vec)
        out_specs += [row, pl.BlockSpec((tm, LANES), lambda i: (i, 0))]
        out_shape += [jax.ShapeDtypeStruct((n, d), BF16), jax.ShapeDtypeStruct((n, LANES), F32)]
    return pl.pallas_call(
        functools.partial(_norm_kernel, d=d, with_out=g_out is not None, with_pair=g_pair is not None),
        grid=(n // tm,),
        in_specs=in_specs, out_specs=out_specs, out_shape=out_shape,
        compiler_params=_params("parallel"),
        name="rmsnorm",
    )(*ins)


FFN_TF = 256
FFN_ROWS = 256


def _ffn_up_kernel(xg_ref, ss_ref, wg_ref, wu_ref, o_ref):
    wg = wg_ref[...].astype(BF16)
    wu = wu_ref[...].astype(BF16)
    d = xg_ref.shape[1]
    tm = xg_ref.shape[0]
    chunk = FFN_ROWS if tm % FFN_ROWS == 0 else tm
    for r0 in range(0, tm, chunk):
        rows = slice(r0, r0 + chunk)
        xg = xg_ref[rows, :]
        s = lax.rsqrt(ss_ref[rows, 0:1] * (1.0 / d) + EPS)
        g = s * jnp.dot(xg, wg, preferred_element_type=F32)
        u = s * jnp.dot(xg, wu, preferred_element_type=F32)
        o_ref[rows, :] = (g * _sigmoid(g) * u).astype(o_ref.dtype)


def _ffn_down_kernel(a_ref, w_ref, x_ref, *rest):
    g_ref, o_ref, xg_ref, ss_ref = rest if len(rest) == 4 else (None, rest[0], None, None)
    acc = jnp.dot(a_ref[...], w_ref[...], preferred_element_type=F32)
    _store_residual(x_ref[...] + 0.5 * acc, g_ref, o_ref, xg_ref, ss_ref)


def _ffn(x2, xg, ss, w_in, w_out, layer, g_next):
    n, d = x2.shape
    f = w_out.shape[1]
    assert w_in.shape[1:] == (d, 2 * f) and f % FFN_TF == 0
    nj = f // FFN_TF
    tm = _tile(n, 2048)
    a = pl.pallas_call(
        _ffn_up_kernel,
        grid=(n // tm, nj),
        in_specs=[pl.BlockSpec((tm, d), lambda i, j: (i, 0), pipeline_mode=pl.Buffered(1)),
                  pl.BlockSpec((tm, LANES), lambda i, j: (i, 0)),
                  pl.BlockSpec((None, d, FFN_TF), lambda i, j: (layer, 0, j)),
                  pl.BlockSpec((None, d, FFN_TF), lambda i, j: (layer, 0, nj + j))],
        out_specs=pl.BlockSpec((tm, FFN_TF), lambda i, j: (i, j)),
        out_shape=jax.ShapeDtypeStruct((n, f), BF16),
        compiler_params=_params("parallel", "arbitrary"),
        name="ffn_up",
    )(xg, ss, w_in, w_in)
    tm2 = _tile(n, 512)
    tn = _tile(d, 512)
    with_norm = g_next is not None
    out_specs, out_shape = _residual_out(n, d, tm2, tn, with_norm)
    ins = [a, w_out, x2] + ([g_next.reshape(1, d)] if with_norm else [])
    in_specs = [pl.BlockSpec((tm2, f), lambda i, j: (i, 0)),
                pl.BlockSpec((None, f, tn), lambda i, j: (layer, 0, j)),
                pl.BlockSpec((tm2, tn), lambda i, j: (i, j))]
    if with_norm:
        in_specs.append(pl.BlockSpec((1, tn), lambda i, j: (0, j)))
    return pl.pallas_call(
        _ffn_down_kernel,
        grid=(n // tm2, d // tn),
        in_specs=in_specs, out_specs=out_specs, out_shape=out_shape,
        compiler_params=_params("parallel", "arbitrary"),
        name="ffn_down",
    )(*ins)


INPROJ_TN = 512


def _in_proj_kernel(pt_ref, qt_ref, xg_ref, ss_ref, w_ref, qg_ref, kg_ref, p_ref, qkv_ref, *,
                    sb_tiles, last_tile, last_valid):
    del pt_ref, qt_ref
    j = pl.program_id(1)
    xg = xg_ref[...]
    acc = _row_scale(ss_ref, xg.shape[1]) * lax.dot_general(
        xg, w_ref[...].astype(BF16), (((1,), (1,)), ((), ())), preferred_element_type=F32)

    def head_norm(gain):
        for c0 in range(0, INPROJ_TN, HEAD_DIM):
            x = acc[:, c0:c0 + HEAD_DIM]
            ms = jnp.mean(x * x, axis=-1, keepdims=True)
            qkv_ref[:, c0:c0 + HEAD_DIM] = (x * lax.rsqrt(ms + EPS) * gain).astype(qkv_ref.dtype)

    @pl.when(j < sb_tiles)
    def _():
        head_norm(qg_ref[...] * SB_SCALE)

    @pl.when(jnp.logical_and(j >= sb_tiles, j < 2 * sb_tiles))
    def _():
        head_norm(kg_ref[...])

    @pl.when(jnp.logical_and(j >= 2 * sb_tiles, j < 3 * sb_tiles))
    def _():
        qkv_ref[...] = acc.astype(qkv_ref.dtype)

    @pl.when(jnp.logical_and(j >= 3 * sb_tiles, j < last_tile))
    def _():
        p_ref[...] = acc

    @pl.when(j == last_tile)
    def _():
        lane = lax.broadcasted_iota(jnp.int32, acc.shape, 1)
        p_ref[...] = jnp.where(lane < last_valid, acc, 0.0)


def _in_proj(xg, ss, w_in, layer, lay, q_gain, k_gain):
    n, d = xg.shape
    src, dst = lay["src"], lay["dst"]
    tn = INPROJ_TN
    nt = -(-lay["in_w"] // tn)
    sb_tiles = lay["sb_w"] // tn
    assert lay["sb_w"] % tn == 0 and src["sb_q"][0] == 0 and src["sb_k"][0] == lay["sb_w"]
    col_dst = np.full(nt * tn, -1)
    for nm in dst:
        o, sz = src[nm]
        col_dst[o:o + sz] = dst[nm] + np.arange(sz)
    p_tile = []
    for t in range(nt):
        cols = col_dst[t * tn:(t + 1) * tn]
        real = np.flatnonzero(cols >= 0)
        first = int(cols[real[0]] - real[0]) if real.size else 0
        assert first % tn == 0 and (cols[real] == first + real).all() and real.size in (0, tn, lay["in_w"] % tn)
        p_tile.append(first // tn)
    q_tile = [min(t, 3 * sb_tiles - 1) for t in range(nt)]
    tm = _tile(n, 2048)
    kern = functools.partial(_in_proj_kernel, sb_tiles=sb_tiles, last_tile=nt - 1,
                             last_valid=lay["in_w"] - (nt - 1) * tn)
    grid_spec = pltpu.PrefetchScalarGridSpec(
        num_scalar_prefetch=2,
        grid=(n // tm, nt),
        in_specs=[pl.BlockSpec((tm, d), lambda i, j, pt, qt: (i, 0), pipeline_mode=pl.Buffered(1)),
                  pl.BlockSpec((tm, LANES), lambda i, j, pt, qt: (i, 0)),
                  pl.BlockSpec((None, tn, d), lambda i, j, pt, qt: (layer, j, 0)),
                  pl.BlockSpec((1, HEAD_DIM), lambda i, j, pt, qt: (0, 0)),
                  pl.BlockSpec((1, HEAD_DIM), lambda i, j, pt, qt: (0, 0))],
        out_specs=[pl.BlockSpec((tm, tn), lambda i, j, pt, qt: (i, pt[j])),
                   pl.BlockSpec((tm, tn), lambda i, j, pt, qt: (i, qt[j]))])
    return pl.pallas_call(
        kern,
        grid_spec=grid_spec,
        out_shape=[jax.ShapeDtypeStruct((n, lay["p_w"]), F32),
                   jax.ShapeDtypeStruct((n, 3 * lay["sb_w"]), BF16)],
        compiler_params=_params("parallel", "arbitrary"),
        name="mix_in_proj",
    )(jnp.asarray(p_tile, jnp.int32), jnp.asarray(q_tile, jnp.int32), xg, ss, w_in,
      q_gain.reshape(1, HEAD_DIM), k_gain.reshape(1, HEAD_DIM))


def _out_proj_kernel(ya_ref, yb_ref, yc_ref, w_ref, x_ref, g_ref, o_ref, xg_ref, ss_ref, y_ref):
    ka, kb = ya_ref.shape[1], yb_ref.shape[1]

    @pl.when(pl.program_id(1) == 0)
    def _():
        y_ref[:, 0:ka] = ya_ref[...]
        y_ref[:, ka:ka + kb] = yb_ref[...]
        y_ref[:, ka + kb:] = yc_ref[...]

    y = x_ref[...] + jnp.dot(y_ref[...], w_ref[...], preferred_element_type=F32)
    _store_residual(y, g_ref, o_ref, xg_ref, ss_ref)


def _out_proj(ya, yb, yc, w, layer, x2, g_next):
    n, d = x2.shape
    tm = _tile(n, 1024)
    tn = _tile(d, 512)
    ka, kb, kc = ya.shape[1], yb.shape[1], yc.shape[1]
    kw = w.shape[1]
    assert ka + kb + kc == kw
    out_specs, out_shape = _residual_out(n, d, tm, tn, True)
    return pl.pallas_call(
        _out_proj_kernel,
        grid=(n // tm, d // tn),
        in_specs=[pl.BlockSpec((tm, ka), lambda i, j: (i, 0)),
                  pl.BlockSpec((tm, kb), lambda i, j: (i, 0)),
                  pl.BlockSpec((tm, kc), lambda i, j: (i, 0)),
                  pl.BlockSpec((None, kw, tn), lambda i, j: (layer, 0, j)),
                  pl.BlockSpec((tm, tn), lambda i, j: (i, j)),
                  pl.BlockSpec((1, tn), lambda i, j: (0, j))],
        out_specs=out_specs, out_shape=out_shape,
        scratch_shapes=[pltpu.VMEM((tm, kw), BF16)],
        compiler_params=_params("parallel", "arbitrary"),
        name="mix_out_proj",
    )(ya, yb, yc, w, x2, g_next.reshape(1, d))


SB_T = 256
SB_HEADS_PER_STEP = 4
SB_DEAD = 105.0


def _sb_attn_kernel(q_ref, k_ref, v_ref, o_ref):
    t = SB_T
    i = pl.program_id(2)
    heads = range(q_ref.shape[1] // HEAD_DIM)
    row = lax.broadcasted_iota(jnp.int32, (t, t), 0)
    col = lax.broadcasted_iota(jnp.int32, (t, t), 1)
    tri = jnp.where(row >= col, 1.0, 0.0).astype(BF16)

    def block(h, kb, carry, limit):
        acc, run = carry
        cols = slice(h * HEAD_DIM, (h + 1) * HEAD_DIM)
        start = pl.multiple_of(kb * t, t)
        k = k_ref[pl.ds(start, t), cols]
        v = v_ref[pl.ds(start, t), cols]
        z = lax.dot_general(q_ref[:, cols], k, (((1,), (1,)), ((), ())), preferred_element_type=F32)
        sp = _softplus(z)
        if limit is not None:
            keep = col < limit
            sp = jnp.where(keep, sp, 0.0)
        hi, lo = _split_bf16(sp, 2)
        c = (jnp.dot(hi, tri, preferred_element_type=F32)
             + jnp.dot(lo, tri, preferred_element_type=F32))
        e = z - c - run
        if limit is not None:
            e = jnp.where(keep, e, NEG_BIG)
        a = jnp.exp(e)
        acc = acc + jnp.dot(a.astype(BF16), v, preferred_element_type=F32)
        return acc, run + c[:, :1]

    carries = [(jnp.zeros((t, HEAD_DIM), F32), jnp.zeros((t, 1), F32)) for _ in heads]
    carries = [block(h, i, carries[h], row) for h in heads]
    carries = [block(h, jnp.maximum(i - 1, 0), carries[h], jnp.where(i >= 1, t, 0)) for h in heads]

    def run_min(carries):
        return functools.reduce(jnp.minimum, [jnp.min(run) for _, run in carries])

    def live(st):
        kb, _, low = st
        return jnp.logical_and(kb >= 0, low < SB_DEAD)

    def step(st):
        kb, carries, _ = st
        carries = [block(h, kb, carries[h], None) for h in heads]
        return kb - 1, carries, run_min(carries)

    _, carries, _ = lax.while_loop(live, step, (i - 2, carries, run_min(carries)))
    for h in heads:
        o_ref[:, h * HEAD_DIM:(h + 1) * HEAD_DIM] = carries[h][0].astype(o_ref.dtype)


def _sb_attention(qkv, batch, seq, heads):
    w = heads * HEAD_DIM
    q3 = qkv.reshape(batch, seq, 3 * w)
    k3 = v3 = q3
    hps = SB_HEADS_PER_STEP if heads % SB_HEADS_PER_STEP == 0 else 1
    groups, gw = heads // hps, hps * HEAD_DIM
    assert seq % SB_T == 0
    out = pl.pallas_call(
        _sb_attn_kernel,
        grid=(batch, groups, seq // SB_T),
        in_specs=[pl.BlockSpec((None, SB_T, gw), lambda b, g, i: (b, i, g)),
                  pl.BlockSpec((None, seq, gw), lambda b, g, i: (b, 0, groups + g)),
                  pl.BlockSpec((None, seq, gw), lambda b, g, i: (b, 0, 2 * groups + g))],
        out_specs=pl.BlockSpec((None, SB_T, gw), lambda b, g, i: (b, i, g)),
        out_shape=jax.ShapeDtypeStruct((batch, seq, w), BF16),
        compiler_params=_params("parallel", "parallel", "arbitrary"),
        name="sb_attention",
    )(q3, k3, v3)
    return out.reshape(batch * seq, w)


CONV_TS = 512
CONV_HALO = 32
CONV_ROWS = 32


def _conv_kernel(val_ref, gate_ref, hval_ref, hgate_ref, w_ref, b_ref, lg_ref, lb_ref, o_ref,
                 u_ref, us_ref):
    ts = o_ref.shape[0]
    i = pl.program_id(1)
    halo = hval_ref[...] * _sigmoid(hgate_ref[...])
    u_ref[0:CONV_HALO, :] = jnp.where(i > 0, halo, 0.0)
    u_ref[CONV_HALO:, :] = val_ref[...] * _sigmoid(gate_ref[...])
    span = CONV_HALO + ts - SUBLANES
    for m in range(1, SUBLANES):
        us_ref[m - 1, 0:span, :] = u_ref[m:m + span, :]
    base = CONV_HALO - (CONV_K - 1)
    bias, lg, lb = b_ref[...], lg_ref[...], lb_ref[...]
    width = o_ref.shape[1]
    for r0 in range(0, ts, CONV_ROWS):
        acc = jnp.zeros((CONV_ROWS // SUBLANES, SUBLANES, width), F32)
        for k in range(CONV_K):
            vreg_row, m = divmod(base + k, SUBLANES)
            lo = vreg_row * SUBLANES + r0
            src = u_ref[lo:lo + CONV_ROWS, :] if m == 0 else us_ref[m - 1, lo:lo + CONV_ROWS, :]
            acc = acc + w_ref[k] * src.reshape(acc.shape)
        acc = acc.reshape(CONV_ROWS, width) + bias
        mu = jnp.mean(acc, axis=-1, keepdims=True)
        cen = acc - mu
        var = jnp.mean(cen * cen, axis=-1, keepdims=True)
        y = cen * lax.rsqrt(var + EPS) * lg + lb
        o_ref[r0:r0 + CONV_ROWS, :] = (y * _sigmoid(y)).astype(o_ref.dtype)


def _conv_module(p, batch, seq, val_off, gate_off, width, conv_w, conv_b, ln_g, ln_b):
    p3 = p.reshape(batch, seq, p.shape[1])
    ts = _tile(seq, CONV_TS)
    assert val_off % width == 0 and gate_off % width == 0 and ts % CONV_HALO == 0
    cv, cg = val_off // width, gate_off // width
    hpt = ts // CONV_HALO
    main = lambda c: pl.BlockSpec((None, ts, width), lambda b, i: (b, i, c))
    halo = lambda c: pl.BlockSpec((None, CONV_HALO, width),
                                  lambda b, i: (b, jnp.maximum(i * hpt - 1, 0), c))
    vec = lambda r: pl.BlockSpec((r, width), lambda b, i: (0, 0))
    out = pl.pallas_call(
        _conv_kernel,
        grid=(batch, seq // ts),
        in_specs=[main(cv), main(cg), halo(cv), halo(cg),
                  pl.BlockSpec((CONV_K, SUBLANES, width), lambda b, i: (0, 0, 0)), vec(1), vec(1), vec(1)],
        out_specs=pl.BlockSpec((None, ts, width), lambda b, i: (b, i, 0)),
        out_shape=jax.ShapeDtypeStruct((batch, seq, width), BF16),
        scratch_shapes=[pltpu.VMEM((CONV_HALO + ts, width), F32),
                        pltpu.VMEM((SUBLANES - 1, CONV_HALO + ts, width), F32)],
        compiler_params=_params("parallel", "arbitrary"),
        name="conv_module",
    )(p3, p3, p3, p3, jnp.broadcast_to(conv_w[:, None, :], (CONV_K, SUBLANES, width)),
      conv_b.reshape(1, width), ln_g.reshape(1, width), ln_b.reshape(1, width))
    return out.reshape(batch * seq, width)


GLA_ROWS = 512
GLA_SUB = 16


def _gla_kernel(q_ref, k_ref, v_ref, r_ref, lr_ref, gw_ref, gb_ref, og_ref, o_ref, state_ref):
    c, sb = GLA_CHUNK, GLA_SUB
    rows = q_ref.shape[0]
    nb, spc = rows // sb, c // sb

    @pl.when(pl.program_id(2) == 0)
    def _():
        state_ref[...] = jnp.zeros_like(state_ref)

    lr_parts = _split_bf16(lr_ref[...], 2)
    gw_parts = _split_bf16(gw_ref[...], 2)
    logit = (jnp.dot(lr_parts[0], gw_parts[0], preferred_element_type=F32)
             + jnp.dot(lr_parts[1], gw_parts[0], preferred_element_type=F32)
             + jnp.dot(lr_parts[0], gw_parts[1], preferred_element_type=F32)) + gb_ref[...]
    g = -_softplus(-logit) * (1.0 / GLA_TAU)

    row = lax.broadcasted_iota(jnp.int32, (c, c), 0)
    col = lax.broadcasted_iota(jnp.int32, (c, c), 1)
    tril = jnp.where(row >= col, 1.0, 0.0).astype(BF16)
    g_parts = _split_bf16(g, 3)
    b = jnp.concatenate(
        [sum(jnp.dot(tril, part[r0:r0 + c, :], preferred_element_type=F32) for part in g_parts)
         for r0 in range(0, rows, c)], axis=0)

    q = q_ref[...] * GLA_SCALE
    k = k_ref[...]
    b3, q3, k3 = (a.reshape(nb, sb, GLA_DK) for a in (b, q, k))
    r3 = (b3 - g.reshape(nb, sb, GLA_DK))[:, 0:1, :]
    q_sub = (q3 * jnp.exp(b3 - r3)).reshape(rows, GLA_DK).astype(BF16)

    ones = jnp.ones((GLA_DK, LANES), BF16)
    half = sb // 2
    t_half = lax.broadcasted_iota(jnp.int32, (nb, half, GLA_DK), 1)
    lane = lax.broadcasted_iota(jnp.int32, (nb, half, LANES), 2)
    sub = lax.broadcasted_iota(jnp.int32, (nb, half, LANES), 0) % (c // sb)
    off_lo = off_hi = lane - sub * sb
    halves = lambda a: (a[:, :half, :], a[:, half:, :])
    (b_lo, b_hi), (q_lo, q_hi) = halves(b3), halves(q3)

    def row_sums(qh, bh, s, first_row):
        e = bh - b3[:, s:s + 1, :]
        if s > first_row:
            e = jnp.where(t_half >= s - first_row, e, NEG_BIG)
        prod = (qh * k3[:, s:s + 1, :] * jnp.exp(e)).reshape(nb * half, GLA_DK).astype(BF16)
        return jnp.dot(prod, ones, preferred_element_type=F32).reshape(nb, half, LANES)

    diag_lo = jnp.zeros((nb, half, LANES), F32)
    diag_hi = jnp.zeros((nb, half, LANES), F32)
    for s in range(sb):
        if s < half:
            diag_lo = jnp.where(off_lo == s, row_sums(q_lo, b_lo, s, 0), diag_lo)
        diag_hi = jnp.where(off_hi == s, row_sums(q_hi, b_hi, s, half), diag_hi)
    diag = jnp.concatenate([diag_lo, diag_hi], axis=1).reshape(rows, LANES)

    kidx = lax.broadcasted_iota(jnp.int32, (c, GLA_DK), 0)
    og = og_ref[...]
    for ci in range(rows // c):
        lo = ci * c
        bc, kc = b[lo:lo + c, :], k[lo:lo + c, :]
        vc = v_ref[lo:lo + c, :].astype(BF16)
        parts = [jnp.zeros((sb, c), F32)]
        for i in range(1, spc):
            r_i = r3[ci * spc + i]
            k_i = (kc * jnp.exp(jnp.where(kidx < i * sb, r_i - bc, NEG_BIG))).astype(BF16)
            parts.append(lax.dot_general(q_sub[lo + i * sb:lo + (i + 1) * sb, :], k_i,
                                         (((1,), (1,)), ((), ())), preferred_element_type=F32))
        scores = jnp.concatenate(parts, axis=0) + diag[lo:lo + c, :c]

        state = state_ref[...]
        o = jnp.dot(scores.astype(BF16), vc, preferred_element_type=F32)
        o = o + lax.dot_general((q[lo:lo + c, :] * jnp.exp(bc)).astype(BF16), state.astype(BF16),
                                (((1,), (1,)), ((), ())), preferred_element_type=F32)
        b_last = bc[c - 1:c, :]
        k_dec = (kc * jnp.exp(b_last - bc)).astype(BF16)
        state_ref[...] = jnp.exp(b_last) * state + lax.dot_general(
            vc, k_dec, (((0,), (0,)), ((), ())), preferred_element_type=F32)

        ms = jnp.mean(o * o, axis=-1, keepdims=True)
        o = o * lax.rsqrt(ms + EPS) * og
        gate = r_ref[lo:lo + c, :]
        o_ref[lo:lo + c, :] = (o * gate * _sigmoid(gate)).astype(o_ref.dtype)


def _gla(p, batch, seq, heads, q_off, k_off, v_off, r_off, lr_off, gate_w_p, gate_b, out_gain):
    p3 = p.reshape(batch, seq, p.shape[1])
    rows = _tile(seq, GLA_ROWS)
    assert rows % GLA_CHUNK == 0
    assert q_off % GLA_DK == 0 and k_off % GLA_DK == 0 and lr_off % LANES == 0
    assert v_off % GLA_DV == 0 and r_off % GLA_DV == 0
    spec = lambda width, blk0: pl.BlockSpec((None, rows, width), lambda b, h, i: (b, i, blk0 + h))
    out = pl.pallas_call(
        _gla_kernel,
        grid=(batch, heads, seq // rows),
        in_specs=[spec(GLA_DK, q_off // GLA_DK), spec(GLA_DK, k_off // GLA_DK),
                  spec(GLA_DV, v_off // GLA_DV), spec(GLA_DV, r_off // GLA_DV),
                  pl.BlockSpec((None, rows, LANES), lambda b, h, i: (b, i, lr_off // LANES)),
                  pl.BlockSpec((LANES, GLA_DK), lambda b, h, i: (0, h)),
                  pl.BlockSpec((1, GLA_DK), lambda b, h, i: (0, h)),
                  pl.BlockSpec((1, GLA_DV), lambda b, h, i: (0, 0))],
        out_specs=pl.BlockSpec((None, rows, GLA_DV), lambda b, h, i: (b, i, h)),
        out_shape=jax.ShapeDtypeStruct((batch, seq, heads * GLA_DV), BF16),
        scratch_shapes=[pltpu.VMEM((GLA_DV, GLA_DK), F32)],
        compiler_params=_params("parallel", "parallel", "arbitrary"),
        name="gla",
    )(p3, p3, p3, p3, p3, gate_w_p, gate_b.reshape(1, -1), out_gain.reshape(1, GLA_DV))
    return out.reshape(batch * seq, heads * GLA_DV)


def _layout(d_model):
    sb_w = 3 * d_model // 8
    conv_w = d_model // 4
    gla_w = d_model - sb_w - conv_w
    gla_heads = gla_w // GLA_DV
    key_w = gla_heads * GLA_DK
    names = ("sb_q", "sb_k", "sb_v", "conv_val", "conv_gate", "gla_q", "gla_k", "gla_v", "gla_r", "gla_lr")
    sizes = (sb_w, sb_w, sb_w, conv_w, conv_w, key_w, key_w, gla_w, gla_w, GLA_RANK)
    src, o = {}, 0
    for nm, sz in zip(names, sizes):
        src[nm] = (o, sz)
        o += sz
    order = ("conv_val", "conv_gate", "gla_v", "gla_r", "gla_q", "gla_k", "gla_lr")
    dst, o = {}, 0
    for nm in order:
        dst[nm] = o
        o += src[nm][1]
    o = dst["gla_lr"] + INPROJ_TN
    return dict(sb_w=sb_w, conv_w=conv_w, gla_w=gla_w, sb_heads=sb_w // HEAD_DIM,
                gla_heads=gla_heads, src=src, dst=dst, p_w=o, in_w=sum(sizes))


def kernel(x, ffn1_norm, ffn1_w_in, ffn1_w_out, mix_norm, mix_w_in, sb_q_gain, sb_k_gain, conv_w, conv_b, conv_ln_g, conv_ln_b, gla_gate_w, gla_gate_b, gla_out_gain, mix_w_out, ffn2_norm, ffn2_w_in, ffn2_w_out, final_norm):
    batch, seq, d = x.shape
    depth = ffn1_norm.shape[0]
    lay = _layout(d)
    dst = lay["dst"]
    assert mix_w_in.shape[2] == lay["in_w"]

    ffn1_w_out, ffn2_w_out, mix_w_out = (w.astype(BF16) for w in (ffn1_w_out, ffn2_w_out, mix_w_out))
    mix_w_in = jnp.swapaxes(mix_w_in, 1, 2)
    x2 = x.reshape(batch * seq, d)
    xg, ss = _norm(x2, None, ffn1_norm[0])
    for l in range(depth):
        x2, xg, ss = _ffn(x2, xg, ss, ffn1_w_in, ffn1_w_out, l, mix_norm[l])

        p, qkv = _in_proj(xg, ss, mix_w_in, l, lay, sb_q_gain[l], sb_k_gain[l])
        ya = _sb_attention(qkv, batch, seq, lay["sb_heads"])
        yb = _conv_module(p, batch, seq, dst["conv_val"], dst["conv_gate"], lay["conv_w"],
                          conv_w[l], conv_b[l], conv_ln_g[l], conv_ln_b[l])
        gate_w_p = jnp.pad(gla_gate_w[l], ((0, LANES - GLA_RANK), (0, 0)))
        yc = _gla(p, batch, seq, lay["gla_heads"], dst["gla_q"], dst["gla_k"], dst["gla_v"],
                  dst["gla_r"], dst["gla_lr"], gate_w_p, gla_gate_b[l], gla_out_gain[l])
        x2, xg, ss = _out_proj(ya, yb, yc, mix_w_out, l, x2, ffn2_norm[l])

        (x2,) = _ffn(x2, xg, ss, ffn2_w_in, ffn2_w_out, l, None)
        if l + 1 < depth:
            x2, xg, ss = _norm(x2, final_norm[l], ffn1_norm[l + 1])
        else:
            (x2,) = _norm(x2, final_norm[l], None)
    return x2.reshape(batch, seq, d)
```
